```python
import math, functools
import jax, jax.numpy as jnp
from jax import lax
import numpy as np

D_MODEL = 1024
BATCH = 16
SEQ = 4096
DEPTH = 1
DEC_BATCH = 128
DEC_SEQ = 1
PAST_LEN = 8192
PAGE_SIZE = 128

SSD_EXPAND = 2
D_INNER = SSD_EXPAND * D_MODEL
SSD_HEAD_DIM = 64
N_SSD_HEADS = D_INNER // SSD_HEAD_DIM
N_GROUPS = 8
HEADS_PER_GROUP = N_SSD_HEADS // N_GROUPS
D_STATE = 128
CONV_WIDTH = 4
CONV_DIM = D_INNER + 2 * N_GROUPS * D_STATE
SSD_CHUNK = 128
ATT_HEAD_DIM = 64
N_ATT_HEADS = D_MODEL // ATT_HEAD_DIM
D_ATT = N_ATT_HEADS * ATT_HEAD_DIM
Q_BLOCK = 128
FORGET_BIAS_INIT = 3.0
D_FF = 4 * D_MODEL
EPS = 1e-6
D_IN_TOTAL = D_INNER + CONV_DIM + N_SSD_HEADS + 3 * D_ATT + N_ATT_HEADS + 2 * D_MODEL

kernel_name = "hybrid_ssd_fox_adaln_decoder_step"


def rmsnorm(x, g):
    xf = x.astype(jnp.float32)
    y = xf * lax.rsqrt(jnp.mean(xf * xf, axis=-1, keepdims=True) + EPS)
    return (y * g.astype(jnp.float32)).astype(x.dtype)


def causal_conv(xbc, hist, w, b):
    xh = jnp.concatenate([hist.astype(xbc.dtype), xbc], axis=1)
    L = xbc.shape[1]
    out = b
    for tap in range(CONV_WIDTH):
        out = out + w[tap] * xh[:, tap:tap + L]
    return jax.nn.silu(out), xh[:, L:]


def ssd_chunk_step(state, chunk):
    xc, ac, bc, cc = chunk
    l = xc.shape[1]
    a_cs = jnp.cumsum(ac, axis=1)
    causal = jnp.tril(jnp.ones((l, l), bool))[None, :, :, None, None]
    seg = a_cs[:, :, None] - a_cs[:, None, :]
    decay = jnp.exp(jnp.where(causal, seg, -jnp.inf))
    cb = jnp.einsum('btgn,bsgn->btsg', cc, bc)
    y = jnp.einsum('btsg,btsgr,bsgrp->btgrp', cb, decay, xc)
    y = y + jnp.einsum('btgn,bgrpn->btgrp', cc, state) * jnp.exp(a_cs)[..., None]
    to_end = jnp.exp(a_cs[:, -1:] - a_cs)
    new_state = (state * jnp.exp(a_cs[:, -1])[..., None, None]
                 + jnp.einsum('bsgn,bsgr,bsgrp->bgrpn', bc, to_end, xc))
    return new_state, y


def ssd_scan(x, a, bm, cm, state0):
    bsz, L = x.shape[:2]
    q = min(SSD_CHUNK, L)
    n_chunks = -(-L // q)
    pad = n_chunks * q - L
    f32 = jnp.float32
    x, a, bm, cm = x.astype(f32), a.astype(f32), bm.astype(f32), cm.astype(f32)
    if pad:
        x = jnp.pad(x, ((0, 0), (0, pad), (0, 0), (0, 0)))
        a = jnp.pad(a, ((0, 0), (0, pad), (0, 0)))
        bm = jnp.pad(bm, ((0, 0), (0, pad), (0, 0), (0, 0)))
        cm = jnp.pad(cm, ((0, 0), (0, pad), (0, 0), (0, 0)))
    xs = jnp.moveaxis(x.reshape(bsz, n_chunks, q, N_GROUPS, HEADS_PER_GROUP, SSD_HEAD_DIM), 1, 0)
    as_ = jnp.moveaxis(a.reshape(bsz, n_chunks, q, N_GROUPS, HEADS_PER_GROUP), 1, 0)
    bs = jnp.moveaxis(bm.reshape(bsz, n_chunks, q, N_GROUPS, D_STATE), 1, 0)
    cs = jnp.moveaxis(cm.reshape(bsz, n_chunks, q, N_GROUPS, D_STATE), 1, 0)
    s0 = state0.astype(f32).reshape(bsz, N_GROUPS, HEADS_PER_GROUP, SSD_HEAD_DIM, D_STATE)
    state, y = lax.scan(ssd_chunk_step, s0, (xs, as_, bs, cs))
    y = jnp.moveaxis(y, 0, 1).reshape(bsz, n_chunks * q, N_SSD_HEADS, SSD_HEAD_DIM)[:, :L]
    return y, state.reshape(bsz, N_SSD_HEADS, SSD_HEAD_DIM, D_STATE).astype(state0.dtype)


def fox_prompt(q, k, v, logf):
    bsz, L = q.shape[:2]
    fh = jnp.cumsum(logf.astype(jnp.float32), axis=1).transpose(0, 2, 1)
    scale = ATT_HEAD_DIM ** -0.5
    key_pos = jnp.arange(L)

    def one_block(i):
        start = i * Q_BLOCK
        qb = lax.dynamic_slice_in_dim(q, start, Q_BLOCK, axis=1)
        fq = lax.dynamic_slice_in_dim(fh, start, Q_BLOCK, axis=2)
        s = jnp.einsum('bqhd,bkhd->bhqk', qb, k).astype(jnp.float32) * scale
        s = s + fq[..., :, None] - fh[..., None, :]
        qpos = start + jnp.arange(Q_BLOCK)
        s = jnp.where(key_pos[None, :] <= qpos[:, None], s, -jnp.inf)
        p = jax.nn.softmax(s, axis=-1)
        return jnp.einsum('bhqk,bkhd->bqhd', p, v).astype(q.dtype)

    out = lax.map(one_block, jnp.arange(L // Q_BLOCK))
    return jnp.moveaxis(out, 0, 1).reshape(bsz, L, N_ATT_HEADS, ATT_HEAD_DIM)


def fox_sample(q, k, v, logf, cache_k, cache_v, cache_logf, page_table, layer):
    bsz, S = q.shape[:2]
    n_pages = page_table.shape[1]
    past = n_pages * PAGE_SIZE
    f32 = jnp.float32
    scale = ATT_HEAD_DIM ** -0.5
    logf_past = cache_logf[layer, page_table].reshape(bsz, past, N_ATT_HEADS)
    fcum = jnp.cumsum(jnp.concatenate([logf_past.astype(f32), logf.astype(f32)], axis=1), axis=1)
    fq = fcum[:, past:].transpose(0, 2, 1)
    fpast = jnp.moveaxis(fcum[:, :past].reshape(bsz, n_pages, PAGE_SIZE, N_ATT_HEADS), 1, 0)
    fq_h = fq[..., :, None]
    s = jnp.einsum('bqhd,bkhd->bhqk', q, k).astype(f32) * scale + fq_h - fq[..., None, :]
    s = jnp.where(jnp.tril(jnp.ones((S, S), bool)), s, -jnp.inf)
    m = jnp.max(s, axis=-1, keepdims=True)
    p = jnp.exp(s - m)
    l = jnp.sum(p, axis=-1, keepdims=True)
    acc = jnp.einsum('bhqk,bkhd->bhqd', p, v.astype(f32))

    def page_step(carry, xs):
        m, l, acc = carry
        pt, fp = xs
        kp = cache_k[layer, pt]
        vp = cache_v[layer, pt]
        sp = jnp.einsum('bqhd,bkhd->bhqk', q, kp).astype(f32) * scale
        sp = sp + fq_h - fp.transpose(0, 2, 1)[:, :, None, :]
        m_new = jnp.maximum(m, jnp.max(sp, axis=-1, keepdims=True))
        corr = jnp.exp(m - m_new)
        pp = jnp.exp(sp - m_new)
        l = l * corr + jnp.sum(pp, axis=-1, keepdims=True)
        acc = acc * corr + jnp.einsum('bhqk,bkhd->bhqd', pp, vp.astype(f32))
        return (m_new, l, acc), None

    (m, l, acc), _ = lax.scan(page_step, (m, l, acc), (page_table.T, fpast))
    return (acc / l).transpose(0, 2, 1, 3).astype(q.dtype)


def hybrid_layer(x, c, conv_hist, ssm_state, attend, w_ada, b_ada, g_mix, g_mlp, w_in, conv_w, conv_b,
                 dt_bias, a_log, d_skip, g_ssd_norm, b_forget, w_proj_ssd, w_proj_att, w_out,
                 w_mlp_up, w_mlp_down):
    bsz, L, _ = x.shape
    mod = (c @ w_ada + b_ada)[:, None, :]
    shift_a, scale_a, gate_a, shift_m, scale_m, gate_m = jnp.split(mod, 6, axis=-1)
    u = rmsnorm(x, g_mix) * (1 + scale_a) + shift_a
    proj = u @ w_in
    splits = [int(s) for s in np.cumsum([D_INNER, CONV_DIM, N_SSD_HEADS, D_ATT, D_ATT, D_ATT, N_ATT_HEADS])]
    z, xbc, dt_raw, q, k, v, f_logit, gates = jnp.split(proj, splits, axis=-1)

    xbc, conv_new = causal_conv(xbc, conv_hist, conv_w, conv_b)
    xs, bm, cm = jnp.split(xbc, [D_INNER, D_INNER + N_GROUPS * D_STATE], axis=-1)
    xs_f = xs.astype(jnp.float32).reshape(bsz, L, N_SSD_HEADS, SSD_HEAD_DIM)
    bm = bm.reshape(bsz, L, N_GROUPS, D_STATE)
    cm = cm.reshape(bsz, L, N_GROUPS, D_STATE)
    dt = jax.nn.softplus((dt_raw + dt_bias).astype(jnp.float32))
    a = -jnp.exp(a_log.astype(jnp.float32))
    y_h, ssm_new = ssd_scan(xs_f * dt[..., None], dt * a, bm, cm, ssm_state)
    y = (y_h + d_skip.astype(jnp.float32)[:, None] * xs_f).reshape(bsz, L, D_INNER).astype(x.dtype)
    y_ssd = rmsnorm(y * jax.nn.silu(z), g_ssd_norm)

    logf = jax.nn.log_sigmoid((f_logit + b_forget).astype(jnp.float32))
    q = q.reshape(bsz, L, N_ATT_HEADS, ATT_HEAD_DIM)
    k = k.reshape(bsz, L, N_ATT_HEADS, ATT_HEAD_DIM)
    v = v.reshape(bsz, L, N_ATT_HEADS, ATT_HEAD_DIM)
    y_att = attend(q, k, v, logf).reshape(bsz, L, D_ATT)

    g_s, g_a = jnp.split(jax.nn.sigmoid(gates), 2, axis=-1)
    mixed = g_s * (y_ssd @ w_proj_ssd) + g_a * (y_att @ w_proj_att)
    x = x + gate_a * (mixed @ w_out)

    h = rmsnorm(x, g_mlp) * (1 + scale_m) + shift_m
    x = x + gate_m * (jnp.square(jax.nn.relu(h @ w_mlp_up)) @ w_mlp_down)
    return x, (k, v, logf.astype(x.dtype), conv_new, ssm_new)


def setup_inputs(seed: int = 0) -> dict:
    key = jax.random.key(seed)
    ks = jax.random.split(key, 28)
    f32 = jnp.float32
    n_pages = PAST_LEN // PAGE_SIZE
    n_used = DEC_BATCH * n_pages
    n_phys = (5 * n_used) // 4

    def nrm(k, shape, scale):
        return jax.random.normal(k, shape, f32) * scale

    x_prompt = nrm(ks[0], (BATCH, SEQ, D_MODEL), 1.0)
    x_sample = nrm(ks[1], (DEC_BATCH, DEC_SEQ, D_MODEL), 1.0)
    cache_k = nrm(ks[2], (DEPTH, n_phys, PAGE_SIZE, N_ATT_HEADS, ATT_HEAD_DIM), 1.0)
    cache_v = nrm(ks[3], (DEPTH, n_phys, PAGE_SIZE, N_ATT_HEADS, ATT_HEAD_DIM), 1.0)
    cache_logf = jax.nn.log_sigmoid(FORGET_BIAS_INIT + nrm(ks[4], (DEPTH, n_phys, PAGE_SIZE, N_ATT_HEADS), 1.0))
    state_conv = nrm(ks[5], (DEPTH, DEC_BATCH, CONV_WIDTH - 1, CONV_DIM), 1.0)
    state_ssm = nrm(ks[6], (DEPTH, DEC_BATCH, N_SSD_HEADS, SSD_HEAD_DIM, D_STATE), 0.1)
    page_table = jax.random.permutation(ks[7], n_phys)[:n_used].reshape(DEC_BATCH, n_pages).astype(jnp.int32)
    c_prompt = nrm(ks[8], (BATCH, D_MODEL), 1.0)
    c_sample = nrm(ks[9], (DEC_BATCH, D_MODEL), 1.0)
    w_ada = nrm(ks[10], (DEPTH, D_MODEL, 6 * D_MODEL), 0.2 * D_MODEL ** -0.5)
    b_ada = nrm(ks[11], (DEPTH, 6 * D_MODEL), 0.02)
    g_mix = 1.0 + nrm(ks[12], (DEPTH, D_MODEL), 0.02)
    g_mlp = 1.0 + nrm(ks[13], (DEPTH, D_MODEL), 0.02)
    w_in = nrm(ks[14], (DEPTH, D_MODEL, D_IN_TOTAL), D_MODEL ** -0.5)
    conv_w = nrm(ks[15], (DEPTH, CONV_WIDTH, CONV_DIM), CONV_WIDTH ** -0.5)
    conv_b = nrm(ks[16], (DEPTH, CONV_DIM), 0.02)
    dt0 = jnp.exp(jax.random.uniform(ks[17], (DEPTH, N_SSD_HEADS), f32, math.log(1e-3), math.log(1e-1)))
    dt_bias = dt0 + jnp.log(-jnp.expm1(-dt0))
    a_log = jnp.log(jax.random.uniform(ks[18], (DEPTH, N_SSD_HEADS), f32, 1.0, 16.0))
    d_skip = 1.0 + nrm(ks[19], (DEPTH, N_SSD_HEADS), 0.1)
    g_ssd_norm = 1.0 + nrm(ks[20], (DEPTH, D_INNER), 0.02)
    b_forget = FORGET_BIAS_INIT + nrm(ks[21], (DEPTH, N_ATT_HEADS), 0.5)
    w_proj_ssd = nrm(ks[22], (DEPTH, D_INNER, D_MODEL), D_INNER ** -0.5)
    w_proj_att = nrm(ks[23], (DEPTH, D_ATT, D_MODEL), D_ATT ** -0.5)
    w_out = nrm(ks[24], (DEPTH, D_MODEL, D_MODEL), D_MODEL ** -0.5)
    w_mlp_up = nrm(ks[25], (DEPTH, D_MODEL, D_FF), D_MODEL ** -0.5)
    w_mlp_down = nrm(ks[26], (DEPTH, D_FF, D_MODEL), D_FF ** -0.5)
    g_final = 1.0 + nrm(ks[27], (D_MODEL,), 0.02)
    return {"x_prompt": x_prompt, "x_sample": x_sample, "cache_k": cache_k, "cache_v": cache_v,
            "cache_logf": cache_logf, "state_conv": state_conv, "state_ssm": state_ssm,
            "page_table": page_table, "c_prompt": c_prompt, "c_sample": c_sample,
            "w_ada": w_ada, "b_ada": b_ada, "g_mix": g_mix, "g_mlp": g_mlp, "w_in": w_in,
            "conv_w": conv_w, "conv_b": conv_b, "dt_bias": dt_bias, "a_log": a_log, "d_skip": d_skip,
            "g_ssd_norm": g_ssd_norm, "b_forget": b_forget, "w_proj_ssd": w_proj_ssd,
            "w_proj_att": w_proj_att, "w_out": w_out, "w_mlp_up": w_mlp_up, "w_mlp_down": w_mlp_down,
            "g_final": g_final}


def reference(x_prompt, x_sample, cache_k, cache_v, cache_logf, state_conv, state_ssm, page_table,
              c_prompt, c_sample, w_ada, b_ada, g_mix, g_mlp, w_in, conv_w, conv_b, dt_bias, a_log,
              d_skip, g_ssd_norm, b_forget, w_proj_ssd, w_proj_att, w_out, w_mlp_up, w_mlp_down, g_final):
    def run_layer(x, c, conv_hist, ssm0, attend, i):
        return hybrid_layer(x, c, conv_hist, ssm0, attend, w_ada[i], b_ada[i], g_mix[i], g_mlp[i], w_in[i],
                            conv_w[i], conv_b[i], dt_bias[i], a_log[i], d_skip[i], g_ssd_norm[i],
                            b_forget[i], w_proj_ssd[i], w_proj_att[i], w_out[i], w_mlp_up[i], w_mlp_down[i])

    bp = x_prompt.shape[0]
    xp, xs = x_prompt, x_sample
    st_p, st_s = [], []
    for i in range(DEPTH):
        conv0 = jnp.zeros((bp, CONV_WIDTH - 1, CONV_DIM), x_prompt.dtype)
        ssm0 = jnp.zeros((bp, N_SSD_HEADS, SSD_HEAD_DIM, D_STATE), jnp.float32)
        xp, sp = run_layer(xp, c_prompt, conv0, ssm0, fox_prompt, i)
        attend_s = functools.partial(fox_sample, cache_k=cache_k, cache_v=cache_v, cache_logf=cache_logf,
                                     page_table=page_table, layer=i)
        xs, ss = run_layer(xs, c_sample, state_conv[i], state_ssm[i], attend_s, i)
        st_p.append(sp)
        st_s.append(ss)
    y_prompt = rmsnorm(xp, g_final)
    y_sample = rmsnorm(xs, g_final)
    k_prompt = jnp.stack([s[0] for s in st_p])
    v_prompt = jnp.stack([s[1] for s in st_p])
    logf_prompt = jnp.stack([s[2] for s in st_p])
    conv_prompt = jnp.stack([s[3] for s in st_p])
    ssm_prompt = jnp.stack([s[4] for s in st_p])
    k_sample = jnp.stack([s[0] for s in st_s])
    v_sample = jnp.stack([s[1] for s in st_s])
    logf_sample = jnp.stack([s[2] for s in st_s])
    conv_sample = jnp.stack([s[3] for s in st_s])
    ssm_sample = jnp.stack([s[4] for s in st_s])
    return (y_prompt, y_sample, k_prompt, v_prompt, logf_prompt, conv_prompt, ssm_prompt,
            k_sample, v_sample, logf_sample, conv_sample, ssm_sample)
```

```python
import functools

import numpy as np
import jax
import jax.numpy as jnp
from jax import lax
from jax.experimental import pallas as pl
from jax.experimental.pallas import tpu as pltpu

F32 = jnp.float32
BF16 = jnp.bfloat16

D_MODEL = 1024
D_INNER = 2048
SSD_HEAD_DIM = 64
N_SSD_HEADS = 32
N_GROUPS = 8
D_STATE = 128
CONV_WIDTH = 4
CONV_DIM = D_INNER + 2 * N_GROUPS * D_STATE
ATT_HEAD_DIM = 64
N_ATT_HEADS = 16
D_ATT = 1024
D_FF = 4096
PAGE_SIZE = 128
EPS = 1e-6

LANES = 128
SUBLANES = 8
VMEM_LIMIT = 56 * 1024 * 1024

NEG_BIG = -1e30


def _params(semantics, vmem=VMEM_LIMIT):
    return pltpu.CompilerParams(dimension_semantics=semantics, vmem_limit_bytes=vmem)


def _dot(a, b):
    return jnp.dot(a, b, preferred_element_type=F32)


def _dot_nt(a, b):
    return lax.dot_general(a, b, (((1,), (1,)), ((), ())), preferred_element_type=F32)


def _split3(x):
    hi = x.astype(BF16)
    r = x - hi.astype(F32)
    mid = r.astype(BF16)
    lo = (r - mid.astype(F32)).astype(BF16)
    return hi, mid, lo


def _dot3_l(x, w):
    hi, mid, lo = _split3(x)
    return _dot(hi, w) + _dot(mid, w) + _dot(lo, w)


def _dot3_r(w, x):
    hi, mid, lo = _split3(x)
    return _dot(w, hi) + _dot(w, mid) + _dot(w, lo)


def _softplus(x):
    return jnp.maximum(x, 0.0) + jnp.log(1.0 + jnp.exp(-jnp.abs(x)))


def _sigmoid(x):
    return 1.0 / (1.0 + jnp.exp(-x))


def _silu(x):
    return x * _sigmoid(x)


def _rms(x):
    return x * lax.rsqrt(jnp.mean(x * x, axis=-1, keepdims=True) + EPS)


def _const_spec(shape):
    nd = len(shape)
    return pl.BlockSpec(shape, lambda *_: (0,) * nd, pipeline_mode=pl.Buffered(1))


def _mod_kernel(c_ref, w_ref, b_ref, o_ref):
    o_ref[...] = jnp.dot(c_ref[...], w_ref[...], precision=lax.Precision.HIGHEST,
                         preferred_element_type=F32) + b_ref[...]


def _mod_call(c_all, w_ada, b_ada):
    n = c_all.shape[0]
    nblk = w_ada.shape[1] // D_MODEL
    return pl.pallas_call(
        _mod_kernel,
        out_shape=jax.ShapeDtypeStruct((n, w_ada.shape[1]), F32),
        grid=(nblk,),
        in_specs=[pl.BlockSpec((n, D_MODEL), lambda j: (0, 0)),
                  pl.BlockSpec((D_MODEL, D_MODEL), lambda j: (0, j)),
                  pl.BlockSpec((1, D_MODEL), lambda j: (0, j))],
        out_specs=pl.BlockSpec((n, D_MODEL), lambda j: (0, j)),
        compiler_params=_params(("arbitrary",)),
        name="adaln_mod",
    )(c_all, w_ada, b_ada.reshape(1, -1))


def _inproj_ssd_kernel(*refs, tm, seq):
    if seq:
        (x_ref, scale_ref, shift_ref, g_ref, wz_ref, wx_ref, wdt_ref, cw_ref, cb_ref, dtb_ref,
         z_ref, xbc_ref, dt_ref, cnew_ref, pre_ref) = refs
    else:
        (x_ref, scale_ref, shift_ref, g_ref, wz_ref, wx_ref, wdt_ref, cw_ref, cb_ref, dtb_ref, hist_ref,
         z_ref, xbc_ref, dt_ref, cnew_ref) = refs
    u = (_rms(x_ref[...]) * g_ref[...]) * (1.0 + scale_ref[...]) + shift_ref[...]
    ub = u.astype(BF16)
    z_ref[...] = _dot(ub, wz_ref[...]).astype(z_ref.dtype)
    dt_ref[...] = _softplus(_dot(ub, wdt_ref[...]) + dtb_ref[...])
    pre = _dot(ub, wx_ref[...])
    hw = CONV_WIDTH - 1
    if seq:
        @pl.when(pl.program_id(1) == 0)
        def _():
            pre_ref[0:SUBLANES, :] = jnp.zeros((SUBLANES, CONV_DIM), F32)

        pre_ref[SUBLANES:SUBLANES + tm, :] = pre
        acc = cb_ref[...] + cw_ref[hw:hw + 1, :] * pre
        for tap in range(hw):
            off = SUBLANES - hw + tap
            acc = acc + cw_ref[tap:tap + 1, :] * pre_ref[off:off + tm, :]
        tail = pre_ref[tm + SUBLANES - hw:tm + SUBLANES, :]
        cnew_ref[...] = tail
        pre_ref[SUBLANES - hw:SUBLANES, :] = tail
    else:
        acc = cb_ref[...] + cw_ref[hw:hw + 1, :] * pre
        for tap in range(hw):
            acc = acc + cw_ref[tap:tap + 1, :] * hist_ref[tap]
        for tap in range(1, hw):
            cnew_ref[tap - 1] = hist_ref[tap]
        cnew_ref[hw - 1] = pre
    xbc_ref[...] = _silu(acc).astype(xbc_ref.dtype)


def _inproj_ssd_call(x, scale, shift, g, wz, wx, wdt, cw, cb, dtb, hist, *, tm, seq):
    nb, L, _ = x.shape
    mrows = scale.shape[1]
    mblk = 1 if mrows == 1 else tm
    mod_spec = pl.BlockSpec((None, mblk, D_MODEL), (lambda b, l: (b, 0, 0)) if mrows == 1 else (lambda b, l: (b, l, 0)))
    row = lambda w: pl.BlockSpec((None, tm, w), lambda b, l: (b, l, 0))
    in_specs = [row(D_MODEL), mod_spec, mod_spec, _const_spec((1, D_MODEL)),
                _const_spec((D_MODEL, D_INNER)), _const_spec((D_MODEL, CONV_DIM)), _const_spec((D_MODEL, LANES)),
                _const_spec((CONV_WIDTH, CONV_DIM)), _const_spec((1, CONV_DIM)), _const_spec((1, LANES))]
    args = [x, scale, shift, g, wz, wx, wdt, cw, cb, dtb]
    hw = CONV_WIDTH - 1
    if seq:
        xbc_dtype = BF16
        cnew_shape = jax.ShapeDtypeStruct((nb, hw, CONV_DIM), F32)
        cnew_spec = pl.BlockSpec((None, hw, CONV_DIM), lambda b, l: (b, 0, 0))
        scratch = [pltpu.VMEM((tm + SUBLANES, CONV_DIM), F32)]
    else:
        xbc_dtype = F32
        in_specs.append(pl.BlockSpec((None, hw, tm, CONV_DIM), lambda b, l: (b, 0, l, 0)))
        args.append(hist)
        cnew_shape = jax.ShapeDtypeStruct((nb, hw, L, CONV_DIM), F32)
        cnew_spec = pl.BlockSpec((None, hw, tm, CONV_DIM), lambda b, l: (b, 0, l, 0))
        scratch = []
    return pl.pallas_call(
        functools.partial(_inproj_ssd_kernel, tm=tm, seq=seq),
        out_shape=(jax.ShapeDtypeStruct((nb, L, D_INNER), xbc_dtype),
                   jax.ShapeDtypeStruct((nb, L, CONV_DIM), xbc_dtype),
                   jax.ShapeDtypeStruct((nb, L, LANES), F32),
                   cnew_shape),
        grid=(nb, L // tm),
        in_specs=in_specs,
        out_specs=(row(D_INNER), row(CONV_DIM), row(LANES), cnew_spec),
        scratch_shapes=scratch,
        compiler_params=_params(("arbitrary", "arbitrary")),
        name="inproj_ssd_seq" if seq else "inproj_ssd_rows",
    )(*args)


def _inproj_att_kernel(*refs, tm, seq):
    if seq:
        (x_ref, scale_ref, shift_ref, g_ref, wq_ref, wk_ref, wv_ref, wf_ref, wg_ref, fb_ref, eq_ref, ek_ref,
         qa_ref, ka_ref, va_ref, k_ref, v_ref, gate_ref, lf_ref, fcarry_ref) = refs
    else:
        (x_ref, scale_ref, shift_ref, g_ref, wq_ref, wk_ref, wv_ref, wf_ref, wg_ref, fb_ref,
         q_ref, k_ref, v_ref, gate_ref, lf_ref) = refs
    u = (_rms(x_ref[...]) * g_ref[...]) * (1.0 + scale_ref[...]) + shift_ref[...]
    ub = u.astype(BF16)
    q = _dot(ub, wq_ref[...])
    k = _dot(ub, wk_ref[...])
    v = _dot(ub, wv_ref[...])
    k_ref[...] = k
    v_ref[...] = v
    gate_ref[...] = _sigmoid(_dot(ub, wg_ref[...])).astype(gate_ref.dtype)
    lf = -_softplus(-(_dot(ub, wf_ref[...]) + fb_ref[...]))
    lf_ref[...] = lf
    if not seq:
        q_ref[...] = q
        return

    @pl.when(pl.program_id(1) == 0)
    def _():
        fcarry_ref[...] = jnp.zeros((1, LANES), F32)

    rows = lax.broadcasted_iota(jnp.int32, (tm, tm), 0)
    cols = lax.broadcasted_iota(jnp.int32, (tm, tm), 1)
    tri = jnp.where(rows >= cols, 1.0, 0.0).astype(BF16)
    fcum = _dot3_r(tri, lf) + fcarry_ref[...]
    fcarry_ref[...] = fcum[tm - 1:tm, :]
    hi, mid, lo = _split3(fcum)
    lane = lax.broadcasted_iota(jnp.int32, (tm, LANES), 1)
    nh = N_ATT_HEADS
    pieces = jnp.where(lane < nh, hi,
                       jnp.where(lane < 2 * nh, mid,
                                 jnp.where(lane < 3 * nh, lo,
                                           jnp.where(lane == 3 * nh, 1.0, 0.0).astype(BF16))))
    augq = _dot(pieces, eq_ref[...]).astype(BF16)
    augk = _dot(pieces, ek_ref[...]).astype(BF16)
    qb = q.astype(BF16)
    kb = k.astype(BF16)
    vb = v.astype(BF16)
    low = lane < ATT_HEAD_DIM
    one_even = jnp.where(lane == ATT_HEAD_DIM, 1.0, 0.0).astype(BF16)
    one_odd = jnp.where(lane == 0, 1.0, 0.0).astype(BF16)
    for j in range(N_ATT_HEADS // 2):
        sl = slice(j * LANES, (j + 1) * LANES)
        sl_e = slice(2 * j * LANES, (2 * j + 1) * LANES)
        sl_o = slice((2 * j + 1) * LANES, (2 * j + 2) * LANES)
        qa_ref[2 * j] = jnp.where(low, qb[:, sl], augq[:, sl_e])
        qa_ref[2 * j + 1] = jnp.where(low, augq[:, sl_o], qb[:, sl])
        ka_ref[2 * j] = jnp.where(low, kb[:, sl], augk[:, sl_e])
        ka_ref[2 * j + 1] = jnp.where(low, augk[:, sl_o], kb[:, sl])
        va_ref[2 * j] = jnp.where(low, vb[:, sl], one_even)
        va_ref[2 * j + 1] = jnp.where(low, one_odd, vb[:, sl])


def _aug_placement():
    nh = N_ATT_HEADS
    eq = np.zeros((LANES, 2 * D_ATT), np.float32)
    ek = np.zeros((LANES, 2 * D_ATT), np.float32)
    for h in range(nh):
        base = h * LANES + (ATT_HEAD_DIM if h % 2 == 0 else 0)
        for piece in range(3):
            eq[piece * nh + h, base + piece] = 1.0
            eq[3 * nh, base + 3 + piece] = 1.0
            ek[3 * nh, base + piece] = 1.0
            ek[piece * nh + h, base + 3 + piece] = -1.0
    return jnp.asarray(eq, BF16), jnp.asarray(ek, BF16)


def _inproj_att_call(x, scale, shift, g, wq, wk, wv, wf, wg, fb, *, tm, seq):
    nb, L, _ = x.shape
    mrows = scale.shape[1]
    mblk = 1 if mrows == 1 else tm
    mod_spec = pl.BlockSpec((None, mblk, D_MODEL), (lambda b, l: (b, 0, 0)) if mrows == 1 else (lambda b, l: (b, l, 0)))
    row = lambda w: pl.BlockSpec((None, tm, w), lambda b, l: (b, l, 0))
    in_specs = [row(D_MODEL), mod_spec, mod_spec, _const_spec((1, D_MODEL)),
                _const_spec((D_MODEL, D_ATT)), _const_spec((D_MODEL, D_ATT)), _const_spec((D_MODEL, D_ATT)),
                _const_spec((D_MODEL, LANES)), _const_spec((D_MODEL, 2 * D_MODEL)), _const_spec((1, LANES))]
    args = [x, scale, shift, g, wq, wk, wv, wf, wg, fb]
    f32_row = lambda w: jax.ShapeDtypeStruct((nb, L, w), F32)
    common_shapes = (f32_row(D_ATT), f32_row(D_ATT), jax.ShapeDtypeStruct((nb, L, 2 * D_MODEL), BF16), f32_row(LANES))
    common_specs = (row(D_ATT), row(D_ATT), row(2 * D_MODEL), row(LANES))
    if seq:
        eq, ek = _aug_placement()
        in_specs += [_const_spec((LANES, 2 * D_ATT)), _const_spec((LANES, 2 * D_ATT))]
        args += [eq, ek]
        head = jax.ShapeDtypeStruct((nb, N_ATT_HEADS, L, LANES), BF16)
        head_spec = pl.BlockSpec((None, N_ATT_HEADS, tm, LANES), lambda b, l: (b, 0, l, 0))
        out_shape = (head, head, head) + common_shapes
        out_specs = (head_spec, head_spec, head_spec) + common_specs
        scratch = [pltpu.VMEM((1, LANES), F32)]
    else:
        out_shape = (f32_row(D_ATT),) + common_shapes
        out_specs = (row(D_ATT),) + common_specs
        scratch = []
    return pl.pallas_call(
        functools.partial(_inproj_att_kernel, tm=tm, seq=seq),
        out_shape=out_shape,
        grid=(nb, L // tm),
        in_specs=in_specs,
        out_specs=out_specs,
        scratch_shapes=scratch,
        compiler_params=_params(("arbitrary", "arbitrary")),
        name="inproj_att_seq" if seq else "inproj_att_rows",
    )(*args)


def _fox_prompt_kernel(qa_ref, ka_ref, va_ref, o_ref, m_ref, acc_ref, *, t):
    qi = pl.program_id(2)
    rows = lax.broadcasted_iota(jnp.int32, (t, t), 0)
    cols = lax.broadcasted_iota(jnp.int32, (t, t), 1)
    causal = rows >= cols
    for h in range(2):
        q = qa_ref[h]
        m_ref[...] = jnp.full((t, LANES), NEG_BIG, F32)
        acc_ref[...] = jnp.zeros((t, LANES), F32)

        def step(ki, masked):
            start = pl.multiple_of(ki * t, t)
            s = _dot_nt(q, ka_ref[h, pl.ds(start, t), :])
            if masked:
                s = jnp.where(causal, s, NEG_BIG)
            m_prev = m_ref[...]
            m_new = jnp.maximum(m_prev, jnp.max(s, axis=-1, keepdims=True))
            p = jnp.exp(s - jnp.tile(m_new, (1, t // LANES)))
            acc_ref[...] = jnp.exp(m_prev - m_new) * acc_ref[...] + _dot(p.astype(BF16), va_ref[h, pl.ds(start, t), :])
            m_ref[...] = m_new

        def body(ki, carry):
            step(ki, False)
            return carry

        lax.fori_loop(0, qi, body, 0)
        step(qi, True)
        acc = acc_ref[...]
        lsum = acc[:, ATT_HEAD_DIM:ATT_HEAD_DIM + 1] if h == 0 else acc[:, 0:1]
        res = acc / lsum
        if h == 0:
            even = res
    lane = lax.broadcasted_iota(jnp.int32, (t, LANES), 1)
    o_ref[...] = jnp.where(lane < ATT_HEAD_DIM, even, res).astype(o_ref.dtype)


def _fox_prompt_call(qa, ka, va, *, t):
    nb, nh, L, _ = qa.shape
    return pl.pallas_call(
        functools.partial(_fox_prompt_kernel, t=t),
        out_shape=jax.ShapeDtypeStruct((nb, L, D_ATT), BF16),
        grid=(nb, nh // 2, L // t),
        in_specs=[pl.BlockSpec((None, 2, t, LANES), lambda b, j, i: (b, j, i, 0)),
                  pl.BlockSpec((None, 2, L, LANES), lambda b, j, i: (b, j, 0, 0)),
                  pl.BlockSpec((None, 2, L, LANES), lambda b, j, i: (b, j, 0, 0))],
        out_specs=pl.BlockSpec((None, t, LANES), lambda b, j, i: (b, i, j)),
        scratch_shapes=[pltpu.VMEM((t, LANES), F32), pltpu.VMEM((t, LANES), F32)],
        compiler_params=_params(("arbitrary", "arbitrary", "arbitrary")),
        name="fox_prompt",
    )(qa, ka, va)


def _head_expand():
    e = np.zeros((LANES, D_INNER), np.float32)
    for h in range(N_SSD_HEADS):
        e[h, h * SSD_HEAD_DIM:(h + 1) * SSD_HEAD_DIM] = 1.0
    return jnp.asarray(e, BF16)


def _ssd_prompt_kernel(xs_ref, b_ref, c_ref, dt_ref, z_ref, alog_ref, dskip_ref, gn_ref, e_ref,
                       y_ref, ssm_ref, st_ref, yacc_ref, *, t):
    ci = pl.program_id(1)
    n_pairs = N_SSD_HEADS // 2

    @pl.when(ci == 0)
    def _():
        st_ref[...] = jnp.zeros(st_ref.shape, F32)

    dt = dt_ref[...]
    a = dt * (-jnp.exp(alog_ref[...]))
    rows = lax.broadcasted_iota(jnp.int32, (t, t), 0)
    cols = lax.broadcasted_iota(jnp.int32, (t, t), 1)
    causal = rows >= cols
    tri = jnp.where(causal, 1.0, 0.0).astype(BF16)
    acs = _dot3_r(tri, a)
    acs_t = acs.T
    a_last = acs[t - 1:t, :]
    dec_col = jnp.exp(acs)
    to_end = jnp.exp(acs_t[:, t - 1:t] - acs_t)
    dec_last = jnp.exp(a_last)
    dhi, dmid, _ = _split3(dt)
    dtx = _dot(dhi, e_ref[...]) + _dot(dmid, e_ref[...])
    xs = xs_ref[...].astype(F32)
    xdt = (xs * dtx).astype(BF16)
    lane = lax.broadcasted_iota(jnp.int32, (1, LANES), 1)
    low = lane < SSD_HEAD_DIM
    zero = jnp.zeros((), BF16)
    hpg = N_SSD_HEADS // N_GROUPS
    for g in range(N_GROUPS):
        bg = b_ref[:, g * D_STATE:(g + 1) * D_STATE]
        cg = c_ref[:, g * D_STATE:(g + 1) * D_STATE]
        cb = _dot_nt(cg, bg)
        bg_t = bg.astype(F32).T
        cg_f = cg.astype(F32)
        for q in range(hpg // 2):
            pair = g * (hpg // 2) + q
            lhs, wts = [], []
            for h in (2 * pair, 2 * pair + 1):
                seg = acs[:, h:h + 1] - acs_t[h:h + 1, :]
                lmat = jnp.exp(jnp.where(causal, seg, NEG_BIG))
                lhs.append((cb * lmat).astype(BF16))
                lhs.append((cg_f * dec_col[:, h:h + 1]).astype(BF16))
                wts.append((bg_t * to_end[h:h + 1, :]).astype(BF16))
            xp = xdt[:, pair * LANES:(pair + 1) * LANES]
            xp0 = jnp.where(low, xp, zero)
            xp1 = jnp.where(low, zero, xp)
            st = st_ref[pair]
            stb = st.astype(BF16)
            rhs = jnp.concatenate([xp0, jnp.where(low, stb, zero), xp1, jnp.where(low, zero, stb)], axis=0)
            yacc_ref[:, pair * LANES:(pair + 1) * LANES] = _dot(jnp.concatenate(lhs, axis=1), rhs)
            dec = jnp.where(low, dec_last[:, 2 * pair:2 * pair + 1], dec_last[:, 2 * pair + 1:2 * pair + 2])
            st_ref[pair] = st * dec + _dot(jnp.concatenate(wts, axis=1), jnp.concatenate([xp0, xp1], axis=0))
    y = yacc_ref[...] + dskip_ref[...] * xs
    y = y * _silu(z_ref[...].astype(F32))
    y_ref[...] = (_rms(y) * gn_ref[...]).astype(y_ref.dtype)

    @pl.when(ci == pl.num_programs(1) - 1)
    def _():
        for pair in range(n_pairs):
            s_t = st_ref[pair].T
            ssm_ref[2 * pair] = s_t[:SSD_HEAD_DIM, :]
            ssm_ref[2 * pair + 1] = s_t[SSD_HEAD_DIM:, :]


def _ssd_prompt_call(xbc, dt, z, alog, dskip_x, gn, *, t):
    nb, L, _ = xbc.shape
    nblk_x = D_INNER // (N_GROUPS * D_STATE)
    return pl.pallas_call(
        functools.partial(_ssd_prompt_kernel, t=t),
        out_shape=(jax.ShapeDtypeStruct((nb, L, D_INNER), BF16),
                   jax.ShapeDtypeStruct((nb, N_SSD_HEADS, SSD_HEAD_DIM, D_STATE), F32)),
        grid=(nb, L // t),
        in_specs=[pl.BlockSpec((None, t, D_INNER), lambda b, c: (b, c, 0)),
                  pl.BlockSpec((None, t, N_GROUPS * D_STATE), lambda b, c: (b, c, nblk_x)),
                  pl.BlockSpec((None, t, N_GROUPS * D_STATE), lambda b, c: (b, c, nblk_x + 1)),
                  pl.BlockSpec((None, t, LANES), lambda b, c: (b, c, 0)),
                  pl.BlockSpec((None, t, D_INNER), lambda b, c: (b, c, 0)),
                  _const_spec((1, LANES)), _const_spec((1, D_INNER)), _const_spec((1, D_INNER)),
                  _const_spec((LANES, D_INNER))],
        out_specs=(pl.BlockSpec((None, t, D_INNER), lambda b, c: (b, c, 0)),
                   pl.BlockSpec((None, N_SSD_HEADS, SSD_HEAD_DIM, D_STATE), lambda b, c: (b, 0, 0, 0))),
        scratch_shapes=[pltpu.VMEM((N_SSD_HEADS // 2, D_STATE, LANES), F32), pltpu.VMEM((t, D_INNER), F32)],
        compiler_params=_params(("arbitrary", "arbitrary")),
        name="ssd_prompt",
    )(xbc, xbc, xbc, dt, z, alog, dskip_x, gn, _head_expand())


def _ssd_step_kernel(xbc_ref, dt_ref, z_ref, st_ref, alog_ref, dskip_ref, gn_ref, e_ref,
                     y_ref, snew_ref):
    b = pl.program_id(0)
    row = xbc_ref[pl.ds(b, 1), :]
    xs = row[:, :D_INNER]
    dt = dt_ref[pl.ds(b, 1), :]
    dec = jnp.exp(dt * (-jnp.exp(alog_ref[...])))
    e = e_ref[...]
    dt8 = jnp.broadcast_to(dt, (SUBLANES, LANES))
    dec8 = jnp.broadcast_to(dec, (SUBLANES, LANES))
    dtx = _dot3_l(dt8, e)[0:1, :]
    decx = _dot3_l(dec8, e)[0:1, :]
    xdt = xs * dtx
    rows_per_g = D_INNER // N_GROUPS
    hpg = N_SSD_HEADS // N_GROUPS
    r_i = lax.broadcasted_iota(jnp.int32, (rows_per_g, rows_per_g), 0)
    k_i = lax.broadcasted_iota(jnp.int32, (rows_per_g, rows_per_g), 1)
    eye = r_i == k_i
    rh = lax.broadcasted_iota(jnp.int32, (rows_per_g, LANES), 0) // SSD_HEAD_DIM
    kh = lax.broadcasted_iota(jnp.int32, (rows_per_g, LANES), 1)
    ones = jnp.ones((LANES, D_STATE), BF16)
    ys = []
    for g in range(N_GROUPS):
        bg = row[:, D_INNER + g * D_STATE:D_INNER + (g + 1) * D_STATE]
        cg = row[:, D_INNER + (N_GROUPS + g) * D_STATE:D_INNER + (N_GROUPS + g + 1) * D_STATE]
        cb = jnp.sum(bg * cg, axis=-1, keepdims=True)
        xg = xdt[:, g * rows_per_g:(g + 1) * rows_per_g]
        st = st_ref[g * rows_per_g:(g + 1) * rows_per_g, :]
        c8 = jnp.broadcast_to(cg, (SUBLANES, D_STATE)).astype(BF16)
        y_state = _dot_nt(c8, st.astype(BF16))[0:1, :]
        dec_rows = _dot3_l(jnp.where(kh == rh + g * hpg, jnp.broadcast_to(dec, (rows_per_g, LANES)), 0.0), ones)
        diag_x = jnp.where(eye, jnp.broadcast_to(xg, (rows_per_g, rows_per_g)), 0.0).astype(BF16)
        outer = _dot(diag_x, jnp.broadcast_to(bg, (rows_per_g, D_STATE)).astype(BF16))
        snew_ref[g * rows_per_g:(g + 1) * rows_per_g, :] = st * dec_rows + outer
        ys.append(cb * xg + y_state * decx[:, g * rows_per_g:(g + 1) * rows_per_g])
    y = jnp.concatenate(ys, axis=1) + dskip_ref[...] * xs
    y = y * _silu(z_ref[pl.ds(b, 1), :].astype(F32))
    y_ref[...] = (_rms(y) * gn_ref[...]).astype(y_ref.dtype)


def _ssd_step_call(xbc, dt, z, state, alog, dskip_x, gn):
    nb = xbc.shape[0]
    st2 = state.reshape(nb, N_SSD_HEADS * SSD_HEAD_DIM, D_STATE)
    y, snew = pl.pallas_call(
        _ssd_step_kernel,
        out_shape=(jax.ShapeDtypeStruct((nb, 1, D_INNER), BF16),
                   jax.ShapeDtypeStruct(st2.shape, F32)),
        grid=(nb,),
        in_specs=[_const_spec((nb, CONV_DIM)), _const_spec((nb, LANES)), _const_spec((nb, D_INNER)),
                  pl.BlockSpec((None, D_INNER, D_STATE), lambda b: (b, 0, 0)),
                  _const_spec((1, LANES)), _const_spec((1, D_INNER)), _const_spec((1, D_INNER)),
                  _const_spec((LANES, D_INNER))],
        out_specs=(pl.BlockSpec((None, 1, D_INNER), lambda b: (b, 0, 0)),
                   pl.BlockSpec((None, D_INNER, D_STATE), lambda b: (b, 0, 0))),
        compiler_params=_params(("arbitrary",)),
        name="ssd_step",
    )(xbc, dt, z, st2, alog, dskip_x, gn, _head_expand())
    return y.reshape(nb, D_INNER), snew.reshape(state.shape)


def _fox_sample_kernel(*refs, g_pages):
    pt_ref = refs[0]
    del pt_ref
    q_ref, kn_ref, vn_ref, lfn_ref = refs[1:5]
    k_refs = refs[5:5 + g_pages]
    v_refs = refs[5 + g_pages:5 + 2 * g_pages]
    lf_refs = refs[5 + 2 * g_pages:5 + 3 * g_pages]
    o_ref, m_ref, l_ref, acc_ref, carry_ref = refs[5 + 3 * g_pages:]
    c = pl.program_id(1)
    nh, hd = N_ATT_HEADS, ATT_HEAD_DIM
    q = q_ref[...]

    @pl.when(c == 0)
    def _():
        m_ref[...] = jnp.broadcast_to(jnp.sum(q * kn_ref[...], axis=-1, keepdims=True), (nh, LANES))
        l_ref[...] = jnp.ones((nh, LANES), F32)
        acc_ref[...] = vn_ref[...]
        carry_ref[...] = lfn_ref[...]

    r_i = lax.broadcasted_iota(jnp.int32, (PAGE_SIZE, 2 * PAGE_SIZE), 0)
    c_i = lax.broadcasted_iota(jnp.int32, (PAGE_SIZE, 2 * PAGE_SIZE), 1)
    suffix = jnp.where((r_i > c_i) | (c_i >= PAGE_SIZE), 1.0, 0.0).astype(BF16)
    qb = q.astype(BF16)
    carry = carry_ref[...]
    s_list = []
    e_i = lax.broadcasted_iota(jnp.int32, (nh, nh), 0)
    e_j = lax.broadcasted_iota(jnp.int32, (nh, nh), 1)
    eye = jnp.where(e_i == e_j, 1.0, 0.0).astype(BF16)
    for g in range(g_pages):
        sums = sum(_dot(_dot_nt(eye, piece).astype(BF16), suffix) for piece in _split3(lf_refs[g][...]))
        bias = sums[:, :PAGE_SIZE] + carry
        carry = carry + sums[:, PAGE_SIZE:]
        rows_s = []
        for h in range(nh):
            kh = k_refs[g][:, h, :].astype(BF16)
            q8 = jnp.broadcast_to(qb[h:h + 1, :], (SUBLANES, hd))
            rows_s.append(_dot_nt(q8, kh)[0:1, :])
        s_list.append(jnp.concatenate(rows_s, axis=0) + bias)
    carry_ref[...] = carry
    s = jnp.concatenate(s_list, axis=1)
    m_prev = m_ref[...]
    m_new = jnp.maximum(m_prev, jnp.max(s, axis=-1, keepdims=True))
    p = jnp.exp(s - jnp.tile(m_new, (1, g_pages)))
    corr = jnp.exp(m_prev - m_new)
    l_ref[...] = corr * l_ref[...] + jnp.sum(p, axis=-1, keepdims=True)
    m_ref[...] = m_new
    pb = p.astype(BF16)
    pv_rows = []
    for h in range(nh):
        acc_h = jnp.zeros((SUBLANES, hd), F32)
        for g in range(g_pages):
            p8 = jnp.broadcast_to(pb[h:h + 1, g * PAGE_SIZE:(g + 1) * PAGE_SIZE], (SUBLANES, PAGE_SIZE))
            acc_h = acc_h + _dot(p8, v_refs[g][:, h, :].astype(BF16))
        pv_rows.append(acc_h[0:1, :])
    acc_ref[...] = corr[:, :hd] * acc_ref[...] + jnp.concatenate(pv_rows, axis=0)

    @pl.when(c == pl.num_programs(1) - 1)
    def _():
        o_ref[...] = (acc_ref[...] / l_ref[:, :hd]).astype(o_ref.dtype)


def _fox_sample_call(q, k_new, v_new, lf_new, cache_k, cache_v, cache_logf, page_table, *, g_pages):
    nb, nh, hd = q.shape
    n_pages = page_table.shape[1]
    n_chunks = n_pages // g_pages

    def page_map(g):
        return lambda b, c, pt: (0, pt[b, n_pages - 1 - (c * g_pages + g)], 0, 0, 0)

    def page_map4(g):
        return lambda b, c, pt: (0, pt[b, n_pages - 1 - (c * g_pages + g)], 0, 0)

    tok = pl.BlockSpec((None, nh, hd), lambda b, c, pt: (b, 0, 0))
    in_specs = [tok, tok, tok, pl.BlockSpec((None, nh, LANES), lambda b, c, pt: (b, 0, 0))]
    in_specs += [pl.BlockSpec((None, None, PAGE_SIZE, nh, hd), page_map(g)) for g in range(g_pages)]
    in_specs += [pl.BlockSpec((None, None, PAGE_SIZE, nh, hd), page_map(g)) for g in range(g_pages)]
    in_specs += [pl.BlockSpec((None, None, PAGE_SIZE, nh), page_map4(g)) for g in range(g_pages)]
    grid_spec = pltpu.PrefetchScalarGridSpec(
        num_scalar_prefetch=1,
        grid=(nb, n_chunks),
        in_specs=in_specs,
        out_specs=pl.BlockSpec((None, nh, hd), lambda b, c, pt: (b, 0, 0)),
        scratch_shapes=[pltpu.VMEM((nh, LANES), F32), pltpu.VMEM((nh, LANES), F32),
                        pltpu.VMEM((nh, hd), F32), pltpu.VMEM((nh, LANES), F32)],
    )
    return pl.pallas_call(
        functools.partial(_fox_sample_kernel, g_pages=g_pages),
        out_shape=jax.ShapeDtypeStruct((nb, nh, hd), BF16),
        grid_spec=grid_spec,
        compiler_params=_params(("arbitrary", "arbitrary")),
        name="fox_sample",
    )(page_table, q, k_new, v_new, lf_new, *([cache_k] * g_pages), *([cache_v] * g_pages),
      *([cache_logf] * g_pages))


def _post_kernel(x_ref, yssd_ref, yatt_ref, gate_ref, ga_ref, shm_ref, scm_ref, gm_ref, gmlp_ref, gfin_ref,
                 wps_ref, wpa_ref, wo_ref, wup_ref, wdn_ref, o_ref):
    gates = gate_ref[...]
    g_s = gates[:, :D_MODEL].astype(F32)
    g_a = gates[:, D_MODEL:].astype(F32)
    mixed = g_s * _dot(yssd_ref[...], wps_ref[...]) + g_a * _dot(yatt_ref[...], wpa_ref[...])
    x1 = x_ref[...] + ga_ref[...] * _dot(mixed.astype(BF16), wo_ref[...])
    h = (_rms(x1) * gmlp_ref[...]) * (1.0 + scm_ref[...]) + shm_ref[...]
    up = jnp.maximum(_dot(h.astype(BF16), wup_ref[...]), 0.0)
    x2 = x1 + gm_ref[...] * _dot((up * up).astype(BF16), wdn_ref[...])
    o_ref[...] = _rms(x2) * gfin_ref[...]


def _post_call(x, yssd, yatt, gates, ga, shm, scm, gm, gmlp, gfin, wps, wpa, wo, wup, wdn, *, tm):
    nb, L, _ = x.shape
    mrows = ga.shape[1]
    mblk = 1 if mrows == 1 else tm
    mod_spec = pl.BlockSpec((None, mblk, D_MODEL), (lambda b, l: (b, 0, 0)) if mrows == 1 else (lambda b, l: (b, l, 0)))
    row = lambda w: pl.BlockSpec((None, tm, w), lambda b, l: (b, l, 0))
    return pl.pallas_call(
        _post_kernel,
        out_shape=jax.ShapeDtypeStruct((nb, L, D_MODEL), F32),
        grid=(nb, L // tm),
        in_specs=[row(D_MODEL), row(D_INNER), row(D_ATT), row(2 * D_MODEL),
                  mod_spec, mod_spec, mod_spec, mod_spec, _const_spec((1, D_MODEL)), _const_spec((1, D_MODEL)),
                  _const_spec((D_INNER, D_MODEL)), _const_spec((D_ATT, D_MODEL)), _const_spec((D_MODEL, D_MODEL)),
                  _const_spec((D_MODEL, D_FF)), _const_spec((D_FF, D_MODEL))],
        out_specs=row(D_MODEL),
        compiler_params=_params(("arbitrary", "arbitrary")),
        name="merge_mlp",
    )(x, yssd, yatt, gates, ga, shm, scm, gm, gmlp, gfin, wps, wpa, wo, wup, wdn)


def _pad_lanes(v, fill=0.0):
    v = v.reshape(1, -1).astype(F32)
    return jnp.pad(v, ((0, 0), (0, LANES - v.shape[1])), constant_values=fill)


def kernel(x_prompt, x_sample, cache_k, cache_v, cache_logf, state_conv, state_ssm, page_table, c_prompt, c_sample,
           w_ada, b_ada, g_mix, g_mlp, w_in, conv_w, conv_b, dt_bias, a_log, d_skip, g_ssd_norm, b_forget,
           w_proj_ssd, w_proj_att, w_out, w_mlp_up, w_mlp_down, g_final):
    depth = w_ada.shape[0]
    assert depth == 1, "single-layer step"
    nbp, L, _ = x_prompt.shape
    nbs = x_sample.shape[0]
    assert x_sample.shape[1] == 1, "one new token per sequence"
    i = 0

    w = w_in[i]
    o_z, o_x, o_dt = 0, D_INNER, D_INNER + CONV_DIM
    o_q = o_dt + N_SSD_HEADS
    o_k, o_v, o_f = o_q + D_ATT, o_q + 2 * D_ATT, o_q + 3 * D_ATT
    o_g = o_f + N_ATT_HEADS
    wz = w[:, o_z:o_x].astype(BF16)
    wx = w[:, o_x:o_dt].astype(BF16)
    wdt = jnp.pad(w[:, o_dt:o_q], ((0, 0), (0, LANES - N_SSD_HEADS))).astype(BF16)
    wq = (w[:, o_q:o_k] * (ATT_HEAD_DIM ** -0.5)).astype(BF16)
    wk = w[:, o_k:o_v].astype(BF16)
    wv = w[:, o_v:o_f].astype(BF16)
    wf1 = w[:, o_f:o_g]
    wf = jnp.pad(jnp.concatenate([wf1, wf1, wf1], axis=1), ((0, 0), (0, LANES - 3 * N_ATT_HEADS))).astype(BF16)
    wg = w[:, o_g:].astype(BF16)
    fb = _pad_lanes(jnp.concatenate([b_forget[i]] * 3))
    dtb = _pad_lanes(dt_bias[i])
    alog = _pad_lanes(a_log[i])
    dskip_x = jnp.repeat(d_skip[i].astype(F32), SSD_HEAD_DIM).reshape(1, D_INNER)
    gn = g_ssd_norm[i].reshape(1, D_INNER)
    gmix = g_mix[i].reshape(1, D_MODEL)
    gmlp = g_mlp[i].reshape(1, D_MODEL)
    gfin = g_final.reshape(1, D_MODEL)
    cw = conv_w[i]
    cb = conv_b[i].reshape(1, CONV_DIM)
    wps = w_proj_ssd[i].astype(BF16)
    wpa = w_proj_att[i].astype(BF16)
    wo = w_out[i].astype(BF16)
    wup = w_mlp_up[i].astype(BF16)
    wdn = w_mlp_down[i].astype(BF16)

    mod = _mod_call(jnp.concatenate([c_prompt, c_sample], axis=0), w_ada[i], b_ada[i])
    mod_p = mod[:nbp].reshape(nbp, 6, 1, D_MODEL)
    mod_s = mod[nbp:].reshape(1, nbs, 6, D_MODEL)
    mp = [mod_p[:, j] for j in range(6)]
    ms = [mod_s[:, :, j] for j in range(6)]

    tm_in = min(256, L)
    z_p, xbc_p, dt_p, conv_p = _inproj_ssd_call(x_prompt, mp[1], mp[0], gmix, wz, wx, wdt, cw, cb, dtb, None,
                                                tm=tm_in, seq=True)
    qa, ka, va, k_p, v_p, gates_p, lf_p = _inproj_att_call(x_prompt, mp[1], mp[0], gmix, wq, wk, wv, wf, wg, fb,
                                                           tm=tm_in, seq=True)
    yssd_p, ssm_p = _ssd_prompt_call(xbc_p, dt_p, z_p, alog, dskip_x, gn, t=min(128, L))
    yatt_p = _fox_prompt_call(qa, ka, va, t=min(512, L))
    y_prompt = _post_call(x_prompt, yssd_p, yatt_p, gates_p, mp[2], mp[3], mp[4], mp[5], gmlp, gfin,
                          wps, wpa, wo, wup, wdn, tm=min(512, L))

    xs_rows = x_sample.reshape(1, nbs, D_MODEL)
    hist = jnp.transpose(state_conv[i], (1, 0, 2))[None]
    z_s, xbc_s, dt_s, conv_s = _inproj_ssd_call(xs_rows, ms[1], ms[0], gmix, wz, wx, wdt, cw, cb, dtb, hist,
                                                tm=nbs, seq=False)
    q_s, k_s, v_s, gates_s, lf_s = _inproj_att_call(xs_rows, ms[1], ms[0], gmix, wq, wk, wv, wf, wg, fb,
                                                    tm=nbs, seq=False)
    yssd_s, ssm_s = _ssd_step_call(xbc_s[0], dt_s[0], z_s[0], state_ssm[i], alog, dskip_x, gn)
    logf_s = lf_s[0, :, :N_ATT_HEADS]
    heads = lambda a: a.reshape(nbs, N_ATT_HEADS, ATT_HEAD_DIM)
    lf_rep = jnp.broadcast_to(logf_s[:, :, None], (nbs, N_ATT_HEADS, LANES))
    yatt_s = _fox_sample_call(heads(q_s[0]), heads(k_s[0]), heads(v_s[0]), lf_rep, cache_k, cache_v, cache_logf,
                              page_table, g_pages=min(8, page_table.shape[1]))
    y_sample = _post_call(xs_rows, yssd_s[None], yatt_s.reshape(1, nbs, D_ATT), gates_s, ms[2], ms[3], ms[4], ms[5],
                          gmlp, gfin, wps, wpa, wo, wup, wdn, tm=nbs)

    hshape = (N_ATT_HEADS, ATT_HEAD_DIM)
    return (y_prompt,
            y_sample.reshape(nbs, 1, D_MODEL),
            k_p.reshape(1, nbp, L, *hshape),
            v_p.reshape(1, nbp, L, *hshape),
            lf_p[None, :, :, :N_ATT_HEADS],
            conv_p[None],
            ssm_p[None],
            k_s.reshape(1, nbs, 1, *hshape),
            v_s.reshape(1, nbs, 1, *hshape),
            logf_s.reshape(1, nbs, 1, N_ATT_HEADS),
            jnp.transpose(conv_s[0], (1, 0, 2))[None],
            ssm_s[None])
```

```python
import functools

import numpy as np
import jax
import jax.numpy as jnp
from jax import lax
from jax.experimental import pallas as pl
from jax.experimental.pallas import tpu as pltpu

F32 = jnp.float32
BF16 = jnp.bfloat16

D_MODEL = 1024
D_INNER = 2048
SSD_HEAD_DIM = 64
N_SSD_HEADS = 32
N_GROUPS = 8
D_STATE = 128
CONV_WIDTH = 4
CONV_DIM = D_INNER + 2 * N_GROUPS * D_STATE
ATT_HEAD_DIM = 64
N_ATT_HEADS = 16
D_ATT = 1024
D_FF = 4096
PAGE_SIZE = 128
EPS = 1e-6

LANES = 128
SUBLANES = 8
VMEM_LIMIT = 56 * 1024 * 1024

NEG_BIG = -1e30


def _params(semantics, vmem=VMEM_LIMIT):
    return pltpu.CompilerParams(dimension_semantics=semantics, vmem_limit_bytes=vmem)


def _dot(a, b):
    return jnp.dot(a, b, preferred_element_type=F32)


def _dot_nt(a, b):
    return lax.dot_general(a, b, (((1,), (1,)), ((), ())), preferred_element_type=F32)


def _split3(x):
    hi = x.astype(BF16)
    r = x - hi.astype(F32)
    mid = r.astype(BF16)
    lo = (r - mid.astype(F32)).astype(BF16)
    return hi, mid, lo


def _dot3_l(x, w):
    hi, mid, lo = _split3(x)
    return _dot(hi, w) + _dot(mid, w) + _dot(lo, w)


def _dot3_r(w, x):
    hi, mid, lo = _split3(x)
    return _dot(w, hi) + _dot(w, mid) + _dot(w, lo)


def _softplus(x):
    return jnp.maximum(x, 0.0) + jnp.log(1.0 + jnp.exp(-jnp.abs(x)))


def _sigmoid(x):
    return 1.0 / (1.0 + jnp.exp(-x))


def _silu(x):
    return x * _sigmoid(x)


def _rms(x):
    return x * lax.rsqrt(jnp.mean(x * x, axis=-1, keepdims=True) + EPS)


def _const_spec(shape):
    nd = len(shape)
    return pl.BlockSpec(shape, lambda *_: (0,) * nd, pipeline_mode=pl.Buffered(1))


def _mod_kernel(c_ref, w_ref, b_ref, o_ref):
    o_ref[...] = jnp.dot(c_ref[...], w_ref[...], precision=lax.Precision.HIGHEST,
                         preferred_element_type=F32) + b_ref[...]


def _mod_call(c_all, w_ada, b_ada):
    n = c_all.shape[0]
    nblk = w_ada.shape[1] // D_MODEL
    return pl.pallas_call(
        _mod_kernel,
        out_shape=jax.ShapeDtypeStruct((n, w_ada.shape[1]), F32),
        grid=(nblk,),
        in_specs=[pl.BlockSpec((n, D_MODEL), lambda j: (0, 0)),
                  pl.BlockSpec((D_MODEL, D_MODEL), lambda j: (0, j)),
                  pl.BlockSpec((1, D_MODEL), lambda j: (0, j))],
        out_specs=pl.BlockSpec((n, D_MODEL), lambda j: (0, j)),
        compiler_params=_params(("arbitrary",)),
        name="adaln_mod",
    )(c_all, w_ada, b_ada.reshape(1, -1))


def _inproj_ssd_kernel(*refs, tm, seq):
    if seq:
        (x_ref, scale_ref, shift_ref, g_ref, wz_ref, wx_ref, wdt_ref, cw_ref, cb_ref, dtb_ref,
         z_ref, xbc_ref, dt_ref, cnew_ref, pre_ref) = refs
    else:
        (x_ref, scale_ref, shift_ref, g_ref, wz_ref, wx_ref, wdt_ref, cw_ref, cb_ref, dtb_ref, hist_ref,
         z_ref, xbc_ref, dt_ref, cnew_ref) = refs
    u = (_rms(x_ref[...]) * g_ref[...]) * (1.0 + scale_ref[...]) + shift_ref[...]
    ub = u.astype(BF16)
    z_ref[...] = _dot(ub, wz_ref[...]).astype(z_ref.dtype)
    dt_ref[...] = _softplus(_dot(ub, wdt_ref[...]) + dtb_ref[...])
    pre = _dot(ub, wx_ref[...])
    hw = CONV_WIDTH - 1
    if seq:
        @pl.when(pl.program_id(1) == 0)
        def _():
            pre_ref[0:SUBLANES, :] = jnp.zeros((SUBLANES, CONV_DIM), F32)

        pre_ref[SUBLANES:SUBLANES + tm, :] = pre
        acc = cb_ref[...] + cw_ref[hw:hw + 1, :] * pre
        for tap in range(hw):
            off = SUBLANES - hw + tap
            acc = acc + cw_ref[tap:tap + 1, :] * pre_ref[off:off + tm, :]
        tail = pre_ref[tm + SUBLANES - hw:tm + SUBLANES, :]
        cnew_ref[...] = tail
        pre_ref[SUBLANES - hw:SUBLANES, :] = tail
    else:
        acc = cb_ref[...] + cw_ref[hw:hw + 1, :] * pre
        for tap in range(hw):
            acc = acc + cw_ref[tap:tap + 1, :] * hist_ref[tap]
        for tap in range(1, hw):
            cnew_ref[tap - 1] = hist_ref[tap]
        cnew_ref[hw - 1] = pre
    xbc_ref[...] = _silu(acc).astype(xbc_ref.dtype)


def _inproj_ssd_call(x, scale, shift, g, wz, wx, wdt, cw, cb, dtb, hist, *, tm, seq):
    nb, L, _ = x.shape
    mrows = scale.shape[1]
    mblk = 1 if mrows == 1 else tm
    mod_spec = pl.BlockSpec((None, mblk, D_MODEL), (lambda b, l: (b, 0, 0)) if mrows == 1 else (lambda b, l: (b, l, 0)))
    row = lambda w: pl.BlockSpec((None, tm, w), lambda b, l: (b, l, 0))
    in_specs = [row(D_MODEL), mod_spec, mod_spec, _const_spec((1, D_MODEL)),
                _const_spec((D_MODEL, D_INNER)), _const_spec((D_MODEL, CONV_DIM)), _const_spec((D_MODEL, LANES)),
                _const_spec((CONV_WIDTH, CONV_DIM)), _const_spec((1, CONV_DIM)), _const_spec((1, LANES))]
    args = [x, scale, shift, g, wz, wx, wdt, cw, cb, dtb]
    hw = CONV_WIDTH - 1
    if seq:
        xbc_dtype = BF16
        cnew_shape = jax.ShapeDtypeStruct((nb, hw, CONV_DIM), F32)
        cnew_spec = pl.BlockSpec((None, hw, CONV_DIM), lambda b, l: (b, 0, 0))
        scratch = [pltpu.VMEM((tm + SUBLANES, CONV_DIM), F32)]
    else:
        xbc_dtype = F32
        in_specs.append(pl.BlockSpec((None, hw, tm, CONV_DIM), lambda b, l: (b, 0, l, 0)))
        args.append(hist)
        cnew_shape = jax.ShapeDtypeStruct((nb, hw, L, CONV_DIM), F32)
        cnew_spec = pl.BlockSpec((None, hw, tm, CONV_DIM), lambda b, l: (b, 0, l, 0))
        scratch = []
    return pl.pallas_call(
        functools.partial(_inproj_ssd_kernel, tm=tm, seq=seq),
        out_shape=(jax.ShapeDtypeStruct((nb, L, D_INNER), xbc_dtype),
                   jax.ShapeDtypeStruct((nb, L, CONV_DIM), xbc_dtype),
                   jax.ShapeDtypeStruct((nb, L, LANES), F32),
                   cnew_shape),
        grid=(nb, L // tm),
        in_specs=in_specs,
        out_specs=(row(D_INNER), row(CONV_DIM), row(LANES), cnew_spec),
        scratch_shapes=scratch,
        compiler_params=_params(("arbitrary", "arbitrary")),
        name="inproj_ssd_seq" if seq else "inproj_ssd_rows",
    )(*args)


def _inproj_att_kernel(*refs, tm, seq):
    if seq:
        (x_ref, scale_ref, shift_ref, g_ref, wq_ref, wk_ref, wv_ref, wf_ref, wg_ref, fb_ref, eq_ref, ek_ref,
         qa_ref, ka_ref, va_ref, k_ref, v_ref, gate_ref, lf_ref, fcarry_ref) = refs
    else:
        (x_ref, scale_ref, shift_ref, g_ref, wq_ref, wk_ref, wv_ref, wf_ref, wg_ref, fb_ref,
         q_ref, k_ref, v_ref, gate_ref, lf_ref) = refs
    u = (_rms(x_ref[...]) * g_ref[...]) * (1.0 + scale_ref[...]) + shift_ref[...]
    ub = u.astype(BF16)
    q = _dot(ub, wq_ref[...])
    k = _dot(ub, wk_ref[...])
    v = _dot(ub, wv_ref[...])
    k_ref[...] = k
    v_ref[...] = v
    gate_ref[...] = _sigmoid(_dot(ub, wg_ref[...])).astype(gate_ref.dtype)
    lf = -_softplus(-(_dot(ub, wf_ref[...]) + fb_ref[...]))
    lf_ref[...] = lf
    if not seq:
        q_ref[...] = q
        return

    @pl.when(pl.program_id(1) == 0)
    def _():
        fcarry_ref[...] = jnp.zeros((1, LANES), F32)

    rows = lax.broadcasted_iota(jnp.int32, (tm, tm), 0)
    cols = lax.broadcasted_iota(jnp.int32, (tm, tm), 1)
    tri = jnp.where(rows >= cols, 1.0, 0.0).astype(BF16)
    fcum = _dot3_r(tri, lf) + fcarry_ref[...]
    fcarry_ref[...] = fcum[tm - 1:tm, :]
    hi, mid, lo = _split3(fcum)
    lane = lax.broadcasted_iota(jnp.int32, (tm, LANES), 1)
    nh = N_ATT_HEADS
    pieces = jnp.where(lane < nh, hi,
                       jnp.where(lane < 2 * nh, mid,
                                 jnp.where(lane < 3 * nh, lo,
                                           jnp.where(lane == 3 * nh, 1.0, 0.0).astype(BF16))))
    augq = _dot(pieces, eq_ref[...]).astype(BF16)
    augk = _dot(pieces, ek_ref[...]).astype(BF16)
    qb = q.astype(BF16)
    kb = k.astype(BF16)
    vb = v.astype(BF16)
    low = lane < ATT_HEAD_DIM
    one_even = jnp.where(lane == ATT_HEAD_DIM, 1.0, 0.0).astype(BF16)
    one_odd = jnp.where(lane == 0, 1.0, 0.0).astype(BF16)
    for j in range(N_ATT_HEADS // 2):
        sl = slice(j * LANES, (j + 1) * LANES)
        sl_e = slice(2 * j * LANES, (2 * j + 1) * LANES)
        sl_o = slice((2 * j + 1) * LANES, (2 * j + 2) * LANES)
        qa_ref[2 * j] = jnp.where(low, qb[:, sl], augq[:, sl_e])
        qa_ref[2 * j + 1] = jnp.where(low, augq[:, sl_o], qb[:, sl])
        ka_ref[2 * j] = jnp.where(low, kb[:, sl], augk[:, sl_e])
        ka_ref[2 * j + 1] = jnp.where(low, augk[:, sl_o], kb[:, sl])
        va_ref[2 * j] = jnp.where(low, vb[:, sl], one_even)
        va_ref[2 * j + 1] = jnp.where(low, one_odd, vb[:, sl])


def _aug_placement():
    nh = N_ATT_HEADS
    eq = np.zeros((LANES, 2 * D_ATT), np.float32)
    ek = np.zeros((LANES, 2 * D_ATT), np.float32)
    for h in range(nh):
        base = h * LANES + (ATT_HEAD_DIM if h % 2 == 0 else 0)
        for piece in range(3):
            eq[piece * nh + h, base + piece] = 1.0
            eq[3 * nh, base + 3 + piece] = 1.0
            ek[3 * nh, base + piece] = 1.0
            ek[piece * nh + h, base + 3 + piece] = -1.0
    return jnp.asarray(eq, BF16), jnp.asarray(ek, BF16)


def _inproj_att_call(x, scale, shift, g, wq, wk, wv, wf, wg, fb, *, tm, seq):
    nb, L, _ = x.shape
    mrows = scale.shape[1]
    mblk = 1 if mrows == 1 else tm
    mod_spec = pl.BlockSpec((None, mblk, D_MODEL), (lambda b, l: (b, 0, 0)) if mrows == 1 else (lambda b, l: (b, l, 0)))
    row = lambda w: pl.BlockSpec((None, tm, w), lambda b, l: (b, l, 0))
    in_specs = [row(D_MODEL), mod_spec, mod_spec, _const_spec((1, D_MODEL)),
                _const_spec((D_MODEL, D_ATT)), _const_spec((D_MODEL, D_ATT)), _const_spec((D_MODEL, D_ATT)),
                _const_spec((D_MODEL, LANES)), _const_spec((D_MODEL, 2 * D_MODEL)), _const_spec((1, LANES))]
    args = [x, scale, shift, g, wq, wk, wv, wf, wg, fb]
    f32_row = lambda w: jax.ShapeDtypeStruct((nb, L, w), F32)
    common_shapes = (f32_row(D_ATT), f32_row(D_ATT), jax.ShapeDtypeStruct((nb, L, 2 * D_MODEL), BF16), f32_row(LANES))
    common_specs = (row(D_ATT), row(D_ATT), row(2 * D_MODEL), row(LANES))
    if seq:
        eq, ek = _aug_placement()
        in_specs += [_const_spec((LANES, 2 * D_ATT)), _const_spec((LANES, 2 * D_ATT))]
        args += [eq, ek]
        head = jax.ShapeDtypeStruct((nb, N_ATT_HEADS, L, LANES), BF16)
        head_spec = pl.BlockSpec((None, N_ATT_HEADS, tm, LANES), lambda b, l: (b, 0, l, 0))
        out_shape = (head, head, head) + common_shapes
        out_specs = (head_spec, head_spec, head_spec) + common_specs
        scratch = [pltpu.VMEM((1, LANES), F32)]
    else:
        out_shape = (f32_row(D_ATT),) + common_shapes
        out_specs = (row(D_ATT),) + common_specs
        scratch = []
    return pl.pallas_call(
        functools.partial(_inproj_att_kernel, tm=tm, seq=seq),
        out_shape=out_shape,
        grid=(nb, L // tm),
        in_specs=in_specs,
        out_specs=out_specs,
        scratch_shapes=scratch,
        compiler_params=_params(("arbitrary", "arbitrary")),
        name="inproj_att_seq" if seq else "inproj_att_rows",
    )(*args)


def _fox_prompt_kernel(qa_ref, ka_ref, va_ref, o_ref, m_ref, acc_ref, *, t, hp):
    qi = pl.program_id(2)
    rows = lax.broadcasted_iota(jnp.int32, (t, t), 0)
    cols = lax.broadcasted_iota(jnp.int32, (t, t), 1)
    causal = rows >= cols
    m_ref[...] = jnp.full((hp, t, LANES), NEG_BIG, F32)
    acc_ref[...] = jnp.zeros((hp, t, LANES), F32)

    def step(ki, masked):
        start = pl.multiple_of(ki * t, t)
        for h in range(hp):
            s = _dot_nt(qa_ref[h], ka_ref[h, pl.ds(start, t), :])
            if masked:
                s = jnp.where(causal, s, NEG_BIG)
            m_prev = m_ref[h]
            m_new = jnp.maximum(m_prev, jnp.max(s, axis=-1, keepdims=True))
            p = jnp.exp(s - jnp.tile(m_new, (1, t // LANES)))
            acc_ref[h] = jnp.exp(m_prev - m_new) * acc_ref[h] + _dot(p.astype(BF16), va_ref[h, pl.ds(start, t), :])
            m_ref[h] = m_new

    def body(ki, carry):
        step(ki, False)
        return carry

    lax.fori_loop(0, qi, body, 0)
    step(qi, True)
    lane = lax.broadcasted_iota(jnp.int32, (t, LANES), 1)
    for j in range(hp // 2):
        acc_e = acc_ref[2 * j]
        acc_o = acc_ref[2 * j + 1]
        even = acc_e / acc_e[:, ATT_HEAD_DIM:ATT_HEAD_DIM + 1]
        odd = acc_o / acc_o[:, 0:1]
        o_ref[:, j * LANES:(j + 1) * LANES] = jnp.where(lane < ATT_HEAD_DIM, even, odd).astype(o_ref.dtype)


def _fox_prompt_call(qa, ka, va, *, t, hp):
    nb, nh, L, _ = qa.shape
    return pl.pallas_call(
        functools.partial(_fox_prompt_kernel, t=t, hp=hp),
        out_shape=jax.ShapeDtypeStruct((nb, L, D_ATT), BF16),
        grid=(nb, nh // hp, L // t),
        in_specs=[pl.BlockSpec((None, hp, t, LANES), lambda b, j, i: (b, j, i, 0)),
                  pl.BlockSpec((None, hp, L, LANES), lambda b, j, i: (b, j, 0, 0)),
                  pl.BlockSpec((None, hp, L, LANES), lambda b, j, i: (b, j, 0, 0))],
        out_specs=pl.BlockSpec((None, t, (hp // 2) * LANES), lambda b, j, i: (b, i, j)),
        scratch_shapes=[pltpu.VMEM((hp, t, LANES), F32), pltpu.VMEM((hp, t, LANES), F32)],
        compiler_params=_params(("arbitrary", "arbitrary", "arbitrary")),
        name="fox_prompt",
    )(qa, ka, va)


def _head_expand():
    e = np.zeros((LANES, D_INNER), np.float32)
    for h in range(N_SSD_HEADS):
        e[h, h * SSD_HEAD_DIM:(h + 1) * SSD_HEAD_DIM] = 1.0
    return jnp.asarray(e, BF16)


def _ssd_prompt_kernel(xs_ref, b_ref, c_ref, dt_ref, z_ref, alog_ref, dskip_ref, gn_ref, e_ref,
                       y_ref, ssm_ref, st_ref, yacc_ref, *, t):
    ci = pl.program_id(1)
    n_pairs = N_SSD_HEADS // 2

    @pl.when(ci == 0)
    def _():
        st_ref[...] = jnp.zeros(st_ref.shape, F32)

    dt = dt_ref[...]
    a = dt * (-jnp.exp(alog_ref[...]))
    rows = lax.broadcasted_iota(jnp.int32, (t, t), 0)
    cols = lax.broadcasted_iota(jnp.int32, (t, t), 1)
    causal = rows >= cols
    tri = jnp.where(causal, 1.0, 0.0).astype(BF16)
    acs = _dot3_r(tri, a)
    acs_t = acs.T
    a_last = acs[t - 1:t, :]
    dec_col = jnp.exp(acs)
    to_end = jnp.exp(acs_t[:, t - 1:t] - acs_t)
    dec_last = jnp.exp(a_last)
    dhi, dmid, _ = _split3(dt)
    dtx = _dot(dhi, e_ref[...]) + _dot(dmid, e_ref[...])
    xs = xs_ref[...].astype(F32)
    xdt = (xs * dtx).astype(BF16)
    lane = lax.broadcasted_iota(jnp.int32, (1, LANES), 1)
    low = lane < SSD_HEAD_DIM
    zero = jnp.zeros((), BF16)
    hpg = N_SSD_HEADS // N_GROUPS
    for g in range(N_GROUPS):
        bg = b_ref[:, g * D_STATE:(g + 1) * D_STATE]
        cg = c_ref[:, g * D_STATE:(g + 1) * D_STATE]
        cb = _dot_nt(cg, bg)
        bg_t = bg.astype(F32).T
        cg_f = cg.astype(F32)
        for q in range(hpg // 2):
            pair = g * (hpg // 2) + q
            lhs, wts = [], []
            for h in (2 * pair, 2 * pair + 1):
                seg = acs[:, h:h + 1] - acs_t[h:h + 1, :]
                lmat = jnp.exp(jnp.where(causal, seg, NEG_BIG))
                lhs.append((cb * lmat).astype(BF16))
                lhs.append((cg_f * dec_col[:, h:h + 1]).astype(BF16))
                wts.append((bg_t * to_end[h:h + 1, :]).astype(BF16))
            xp = xdt[:, pair * LANES:(pair + 1) * LANES]
            xp0 = jnp.where(low, xp, zero)
            xp1 = jnp.where(low, zero, xp)
            st = st_ref[pair]
            stb = st.astype(BF16)
            rhs = jnp.concatenate([xp0, jnp.where(low, stb, zero), xp1, jnp.where(low, zero, stb)], axis=0)
            yacc_ref[:, pair * LANES:(pair + 1) * LANES] = _dot(jnp.concatenate(lhs, axis=1), rhs)
            dec = jnp.where(low, dec_last[:, 2 * pair:2 * pair + 1], dec_last[:, 2 * pair + 1:2 * pair + 2])
            st_ref[pair] = st * dec + _dot(jnp.concatenate(wts, axis=1), jnp.concatenate([xp0, xp1], axis=0))
    y = yacc_ref[...] + dskip_ref[...] * xs
    y = y * _silu(z_ref[...].astype(F32))
    y_ref[...] = (_rms(y) * gn_ref[...]).astype(y_ref.dtype)

    @pl.when(ci == pl.num_programs(1) - 1)
    def _():
        for pair in range(n_pairs):
            s_t = st_ref[pair].T
            ssm_ref[2 * pair] = s_t[:SSD_HEAD_DIM, :]
            ssm_ref[2 * pair + 1] = s_t[SSD_HEAD_DIM:, :]


def _ssd_prompt_call(xbc, dt, z, alog, dskip_x, gn, *, t):
    nb, L, _ = xbc.shape
    nblk_x = D_INNER // (N_GROUPS * D_STATE)
    return pl.pallas_call(
        functools.partial(_ssd_prompt_kernel, t=t),
        out_shape=(jax.ShapeDtypeStruct((nb, L, D_INNER), BF16),
                   jax.ShapeDtypeStruct((nb, N_SSD_HEADS, SSD_HEAD_DIM, D_STATE), F32)),
        grid=(nb, L // t),
        in_specs=[pl.BlockSpec((None, t, D_INNER), lambda b, c: (b, c, 0)),
                  pl.BlockSpec((None, t, N_GROUPS * D_STATE), lambda b, c: (b, c, nblk_x)),
                  pl.BlockSpec((None, t, N_GROUPS * D_STATE), lambda b, c: (b, c, nblk_x + 1)),
                  pl.BlockSpec((None, t, LANES), lambda b, c: (b, c, 0)),
                  pl.BlockSpec((None, t, D_INNER), lambda b, c: (b, c, 0)),
                  _const_spec((1, LANES)), _const_spec((1, D_INNER)), _const_spec((1, D_INNER)),
                  _const_spec((LANES, D_INNER))],
        out_specs=(pl.BlockSpec((None, t, D_INNER), lambda b, c: (b, c, 0)),
                   pl.BlockSpec((None, N_SSD_HEADS, SSD_HEAD_DIM, D_STATE), lambda b, c: (b, 0, 0, 0))),
        scratch_shapes=[pltpu.VMEM((N_SSD_HEADS // 2, D_STATE, LANES), F32), pltpu.VMEM((t, D_INNER), F32)],
        compiler_params=_params(("arbitrary", "arbitrary")),
        name="ssd_prompt",
    )(xbc, xbc, xbc, dt, z, alog, dskip_x, gn, _head_expand())


def _ssd_step_kernel(xbc_ref, dt_ref, z_ref, st_ref, alog_ref, dskip_ref, gn_ref, e_ref,
                     y_ref, snew_ref):
    b = pl.program_id(0)
    row = xbc_ref[pl.ds(b, 1), :]
    xs = row[:, :D_INNER]
    dt = dt_ref[pl.ds(b, 1), :]
    dec = jnp.exp(dt * (-jnp.exp(alog_ref[...])))
    e = e_ref[...]
    dt8 = jnp.broadcast_to(dt, (SUBLANES, LANES))
    dec8 = jnp.broadcast_to(dec, (SUBLANES, LANES))
    dtx = _dot3_l(dt8, e)[0:1, :]
    decx = _dot3_l(dec8, e)[0:1, :]
    xdt = xs * dtx
    rows_per_g = D_INNER // N_GROUPS
    hpg = N_SSD_HEADS // N_GROUPS
    r_i = lax.broadcasted_iota(jnp.int32, (rows_per_g, rows_per_g), 0)
    k_i = lax.broadcasted_iota(jnp.int32, (rows_per_g, rows_per_g), 1)
    eye = r_i == k_i
    rh = lax.broadcasted_iota(jnp.int32, (rows_per_g, LANES), 0) // SSD_HEAD_DIM
    kh = lax.broadcasted_iota(jnp.int32, (rows_per_g, LANES), 1)
    ones = jnp.ones((LANES, D_STATE), BF16)
    ys = []
    for g in range(N_GROUPS):
        bg = row[:, D_INNER + g * D_STATE:D_INNER + (g + 1) * D_STATE]
        cg = row[:, D_INNER + (N_GROUPS + g) * D_STATE:D_INNER + (N_GROUPS + g + 1) * D_STATE]
        cb = jnp.sum(bg * cg, axis=-1, keepdims=True)
        xg = xdt[:, g * rows_per_g:(g + 1) * rows_per_g]
        st = st_ref[g * rows_per_g:(g + 1) * rows_per_g, :]
        c8 = jnp.broadcast_to(cg, (SUBLANES, D_STATE)).astype(BF16)
        y_state = _dot_nt(c8, st.astype(BF16))[0:1, :]
        dec_rows = _dot3_l(jnp.where(kh == rh + g * hpg, jnp.broadcast_to(dec, (rows_per_g, LANES)), 0.0), ones)
        diag_x = jnp.where(eye, jnp.broadcast_to(xg, (rows_per_g, rows_per_g)), 0.0).astype(BF16)
        outer = _dot(diag_x, jnp.broadcast_to(bg, (rows_per_g, D_STATE)).astype(BF16))
        snew_ref[g * rows_per_g:(g + 1) * rows_per_g, :] = st * dec_rows + outer
        ys.append(cb * xg + y_state * decx[:, g * rows_per_g:(g + 1) * rows_per_g])
    y = jnp.concatenate(ys, axis=1) + dskip_ref[...] * xs
    y = y * _silu(z_ref[pl.ds(b, 1), :].astype(F32))
    y_ref[...] = (_rms(y) * gn_ref[...]).astype(y_ref.dtype)


def _ssd_step_call(xbc, dt, z, state, alog, dskip_x, gn):
    nb = xbc.shape[0]
    st2 = state.reshape(nb, N_SSD_HEADS * SSD_HEAD_DIM, D_STATE)
    y, snew = pl.pallas_call(
        _ssd_step_kernel,
        out_shape=(jax.ShapeDtypeStruct((nb, 1, D_INNER), BF16),
                   jax.ShapeDtypeStruct(st2.shape, F32)),
        grid=(nb,),
        in_specs=[_const_spec((nb, CONV_DIM)), _const_spec((nb, LANES)), _const_spec((nb, D_INNER)),
                  pl.BlockSpec((None, D_INNER, D_STATE), lambda b: (b, 0, 0)),
                  _const_spec((1, LANES)), _const_spec((1, D_INNER)), _const_spec((1, D_INNER)),
                  _const_spec((LANES, D_INNER))],
        out_specs=(pl.BlockSpec((None, 1, D_INNER), lambda b: (b, 0, 0)),
                   pl.BlockSpec((None, D_INNER, D_STATE), lambda b: (b, 0, 0))),
        compiler_params=_params(("arbitrary",)),
        name="ssd_step",
    )(xbc, dt, z, st2, alog, dskip_x, gn, _head_expand())
    return y.reshape(nb, D_INNER), snew.reshape(state.shape)


def _fox_sample_kernel(*refs, g_pages):
    q_ref, kn_ref, vn_ref, lfn_ref = refs[1:5]
    k_refs = refs[5:5 + g_pages]
    v_refs = refs[5 + g_pages:5 + 2 * g_pages]
    lf_refs = refs[5 + 2 * g_pages:5 + 3 * g_pages]
    o_ref, qblk_ref, m_ref, l_ref, acc_ref, carry_ref = refs[5 + 3 * g_pages:]
    c = pl.program_id(1)
    nh, hd = N_ATT_HEADS, ATT_HEAD_DIM
    row_h = lax.broadcasted_iota(jnp.int32, (nh, D_ATT), 0)
    lane_h = lax.broadcasted_iota(jnp.int32, (nh, D_ATT), 1) // hd
    own = row_h == lane_h
    rep = D_ATT // LANES

    @pl.when(c == 0)
    def _():
        q = q_ref[...]
        qblk_ref[...] = jnp.where(own, jnp.broadcast_to(q, (nh, D_ATT)), 0.0).astype(BF16)
        s_self = jnp.sum(jnp.where(own, jnp.broadcast_to(q * kn_ref[...], (nh, D_ATT)), 0.0), axis=-1, keepdims=True)
        m_ref[...] = jnp.broadcast_to(s_self, (nh, LANES))
        l_ref[...] = jnp.ones((nh, LANES), F32)
        acc_ref[...] = jnp.where(own, jnp.broadcast_to(vn_ref[...], (nh, D_ATT)), 0.0)
        carry_ref[...] = lfn_ref[...]

    r_i = lax.broadcasted_iota(jnp.int32, (PAGE_SIZE, 2 * PAGE_SIZE), 0)
    c_i = lax.broadcasted_iota(jnp.int32, (PAGE_SIZE, 2 * PAGE_SIZE), 1)
    suffix = jnp.where((r_i > c_i) | (c_i >= PAGE_SIZE), 1.0, 0.0).astype(BF16)
    qblk = qblk_ref[...]
    carry = carry_ref[...]
    s_list = []
    for g in range(g_pages):
        sums = _dot3_l(lf_refs[g][...], suffix)
        s_list.append(_dot(qblk, k_refs[g][...].astype(BF16)) + sums[:, :PAGE_SIZE] + carry)
        carry = carry + sums[:, PAGE_SIZE:]
    carry_ref[...] = carry
    s = jnp.concatenate(s_list, axis=1)
    m_prev = m_ref[...]
    m_new = jnp.maximum(m_prev, jnp.max(s, axis=-1, keepdims=True))
    p = jnp.exp(s - jnp.tile(m_new, (1, g_pages)))
    corr = jnp.exp(m_prev - m_new)
    l_ref[...] = corr * l_ref[...] + jnp.sum(p, axis=-1, keepdims=True)
    m_ref[...] = m_new
    pb = p.astype(BF16)
    pv = _dot_nt(pb[:, :PAGE_SIZE], v_refs[0][...].astype(BF16))
    for g in range(1, g_pages):
        pv = pv + _dot_nt(pb[:, g * PAGE_SIZE:(g + 1) * PAGE_SIZE], v_refs[g][...].astype(BF16))
    acc_ref[...] = jnp.tile(corr, (1, rep)) * acc_ref[...] + pv

    @pl.when(c == pl.num_programs(1) - 1)
    def _():
        a = jnp.where(own, acc_ref[...] / jnp.tile(l_ref[...], (1, rep)), 0.0)
        o_ref[...] = jnp.sum(a, axis=0, keepdims=True).astype(o_ref.dtype)


def _fox_sample_call(q, k_new, v_new, lf_new, cache_kt, cache_vt, cache_lft, page_table, *, g_pages):
    nb = q.shape[0]
    nh = N_ATT_HEADS
    n_pages = page_table.shape[1]
    n_chunks = n_pages // g_pages

    def page_map(g):
        return lambda b, c, pt: (pt[b, n_pages - 1 - (c * g_pages + g)], 0, 0)

    tok = pl.BlockSpec((None, 1, D_ATT), lambda b, c, pt: (b, 0, 0))
    in_specs = [tok, tok, tok, pl.BlockSpec((None, nh, LANES), lambda b, c, pt: (b, 0, 0))]
    in_specs += [pl.BlockSpec((None, D_ATT, PAGE_SIZE), page_map(g)) for g in range(g_pages)]
    in_specs += [pl.BlockSpec((None, D_ATT, PAGE_SIZE), page_map(g)) for g in range(g_pages)]
    in_specs += [pl.BlockSpec((None, nh, PAGE_SIZE), page_map(g)) for g in range(g_pages)]
    grid_spec = pltpu.PrefetchScalarGridSpec(
        num_scalar_prefetch=1,
        grid=(nb, n_chunks),
        in_specs=in_specs,
        out_specs=tok,
        scratch_shapes=[pltpu.VMEM((nh, D_ATT), BF16), pltpu.VMEM((nh, LANES), F32), pltpu.VMEM((nh, LANES), F32),
                        pltpu.VMEM((nh, D_ATT), F32), pltpu.VMEM((nh, LANES), F32)],
    )
    return pl.pallas_call(
        functools.partial(_fox_sample_kernel, g_pages=g_pages),
        out_shape=jax.ShapeDtypeStruct((nb, 1, D_ATT), BF16),
        grid_spec=grid_spec,
        compiler_params=_params(("arbitrary", "arbitrary")),
        name="fox_sample",
    )(page_table, q, k_new, v_new, lf_new, *([cache_kt] * g_pages), *([cache_vt] * g_pages),
      *([cache_lft] * g_pages))


def _post_kernel(x_ref, yssd_ref, yatt_ref, gate_ref, ga_ref, shm_ref, scm_ref, gm_ref, gmlp_ref, gfin_ref,
                 wps_ref, wpa_ref, wo_ref, wup_ref, wdn_ref, o_ref):
    gates = gate_ref[...]
    g_s = gates[:, :D_MODEL].astype(F32)
    g_a = gates[:, D_MODEL:].astype(F32)
    mixed = g_s * _dot(yssd_ref[...], wps_ref[...]) + g_a * _dot(yatt_ref[...], wpa_ref[...])
    x1 = x_ref[...] + ga_ref[...] * _dot(mixed.astype(BF16), wo_ref[...])
    h = (_rms(x1) * gmlp_ref[...]) * (1.0 + scm_ref[...]) + shm_ref[...]
    up = jnp.maximum(_dot(h.astype(BF16), wup_ref[...]), 0.0)
    x2 = x1 + gm_ref[...] * _dot((up * up).astype(BF16), wdn_ref[...])
    o_ref[...] = _rms(x2) * gfin_ref[...]


def _post_call(x, yssd, yatt, gates, ga, shm, scm, gm, gmlp, gfin, wps, wpa, wo, wup, wdn, *, tm):
    nb, L, _ = x.shape
    mrows = ga.shape[1]
    mblk = 1 if mrows == 1 else tm
    mod_spec = pl.BlockSpec((None, mblk, D_MODEL), (lambda b, l: (b, 0, 0)) if mrows == 1 else (lambda b, l: (b, l, 0)))
    row = lambda w: pl.BlockSpec((None, tm, w), lambda b, l: (b, l, 0))
    return pl.pallas_call(
        _post_kernel,
        out_shape=jax.ShapeDtypeStruct((nb, L, D_MODEL), F32),
        grid=(nb, L // tm),
        in_specs=[row(D_MODEL), row(D_INNER), row(D_ATT), row(2 * D_MODEL),
                  mod_spec, mod_spec, mod_spec, mod_spec, _const_spec((1, D_MODEL)), _const_spec((1, D_MODEL)),
                  _const_spec((D_INNER, D_MODEL)), _const_spec((D_ATT, D_MODEL)), _const_spec((D_MODEL, D_MODEL)),
                  _const_spec((D_MODEL, D_FF)), _const_spec((D_FF, D_MODEL))],
        out_specs=row(D_MODEL),
        compiler_params=_params(("arbitrary", "arbitrary")),
        name="merge_mlp",
    )(x, yssd, yatt, gates, ga, shm, scm, gm, gmlp, gfin, wps, wpa, wo, wup, wdn)


def _pad_lanes(v, fill=0.0):
    v = v.reshape(1, -1).astype(F32)
    return jnp.pad(v, ((0, 0), (0, LANES - v.shape[1])), constant_values=fill)


def kernel(x_prompt, x_sample, cache_k, cache_v, cache_logf, state_conv, state_ssm, page_table, c_prompt, c_sample,
           w_ada, b_ada, g_mix, g_mlp, w_in, conv_w, conv_b, dt_bias, a_log, d_skip, g_ssd_norm, b_forget,
           w_proj_ssd, w_proj_att, w_out, w_mlp_up, w_mlp_down, g_final):
    depth = w_ada.shape[0]
    assert depth == 1, "single-layer step"
    nbp, L, _ = x_prompt.shape
    nbs = x_sample.shape[0]
    assert x_sample.shape[1] == 1, "one new token per sequence"
    i = 0

    w = w_in[i]
    o_z, o_x, o_dt = 0, D_INNER, D_INNER + CONV_DIM
    o_q = o_dt + N_SSD_HEADS
    o_k, o_v, o_f = o_q + D_ATT, o_q + 2 * D_ATT, o_q + 3 * D_ATT
    o_g = o_f + N_ATT_HEADS
    wz = w[:, o_z:o_x].astype(BF16)
    wx = w[:, o_x:o_dt].astype(BF16)
    wdt = jnp.pad(w[:, o_dt:o_q], ((0, 0), (0, LANES - N_SSD_HEADS))).astype(BF16)
    wq = (w[:, o_q:o_k] * (ATT_HEAD_DIM ** -0.5)).astype(BF16)
    wk = w[:, o_k:o_v].astype(BF16)
    wv = w[:, o_v:o_f].astype(BF16)
    wf1 = w[:, o_f:o_g]
    wf = jnp.pad(jnp.concatenate([wf1, wf1, wf1], axis=1), ((0, 0), (0, LANES - 3 * N_ATT_HEADS))).astype(BF16)
    wg = w[:, o_g:].astype(BF16)
    fb = _pad_lanes(jnp.concatenate([b_forget[i]] * 3))
    dtb = _pad_lanes(dt_bias[i])
    alog = _pad_lanes(a_log[i])
    dskip_x = jnp.repeat(d_skip[i].astype(F32), SSD_HEAD_DIM).reshape(1, D_INNER)
    gn = g_ssd_norm[i].reshape(1, D_INNER)
    gmix = g_mix[i].reshape(1, D_MODEL)
    gmlp = g_mlp[i].reshape(1, D_MODEL)
    gfin = g_final.reshape(1, D_MODEL)
    cw = conv_w[i]
    cb = conv_b[i].reshape(1, CONV_DIM)
    wps = w_proj_ssd[i].astype(BF16)
    wpa = w_proj_att[i].astype(BF16)
    wo = w_out[i].astype(BF16)
    wup = w_mlp_up[i].astype(BF16)
    wdn = w_mlp_down[i].astype(BF16)

    mod = _mod_call(jnp.concatenate([c_prompt, c_sample], axis=0), w_ada[i], b_ada[i])
    mod_p = mod[:nbp].reshape(nbp, 6, 1, D_MODEL)
    mod_s = mod[nbp:].reshape(1, nbs, 6, D_MODEL)
    mp = [mod_p[:, j] for j in range(6)]
    ms = [mod_s[:, :, j] for j in range(6)]

    tm_in = min(256, L)
    z_p, xbc_p, dt_p, conv_p = _inproj_ssd_call(x_prompt, mp[1], mp[0], gmix, wz, wx, wdt, cw, cb, dtb, None,
                                                tm=tm_in, seq=True)
    qa, ka, va, k_p, v_p, gates_p, lf_p = _inproj_att_call(x_prompt, mp[1], mp[0], gmix, wq, wk, wv, wf, wg, fb,
                                                           tm=tm_in, seq=True)
    yssd_p, ssm_p = _ssd_prompt_call(xbc_p, dt_p, z_p, alog, dskip_x, gn, t=min(128, L))
    yatt_p = _fox_prompt_call(qa, ka, va, t=min(512, L), hp=4)
    y_prompt = _post_call(x_prompt, yssd_p, yatt_p, gates_p, mp[2], mp[3], mp[4], mp[5], gmlp, gfin,
                          wps, wpa, wo, wup, wdn, tm=min(512, L))

    xs_rows = x_sample.reshape(1, nbs, D_MODEL)
    hist = jnp.transpose(state_conv[i], (1, 0, 2))[None]
    z_s, xbc_s, dt_s, conv_s = _inproj_ssd_call(xs_rows, ms[1], ms[0], gmix, wz, wx, wdt, cw, cb, dtb, hist,
                                                tm=nbs, seq=False)
    q_s, k_s, v_s, gates_s, lf_s = _inproj_att_call(xs_rows, ms[1], ms[0], gmix, wq, wk, wv, wf, wg, fb,
                                                    tm=nbs, seq=False)
    yssd_s, ssm_s = _ssd_step_call(xbc_s[0], dt_s[0], z_s[0], state_ssm[i], alog, dskip_x, gn)
    logf_s = lf_s[0, :, :N_ATT_HEADS]
    lf_rep = jnp.broadcast_to(logf_s[:, :, None], (nbs, N_ATT_HEADS, LANES))
    tok = lambda a: a.reshape(nbs, 1, D_ATT)
    n_phys = cache_k.shape[1]
    cache_kt = jnp.transpose(cache_k[i], (0, 2, 3, 1)).reshape(n_phys, D_ATT, PAGE_SIZE)
    cache_vt = jnp.transpose(cache_v[i], (0, 2, 3, 1)).reshape(n_phys, D_ATT, PAGE_SIZE)
    cache_lft = jnp.transpose(cache_logf[i], (0, 2, 1))
    yatt_s = _fox_sample_call(tok(q_s[0]), tok(k_s[0]), tok(v_s[0]), lf_rep, cache_kt, cache_vt, cache_lft,
                              page_table, g_pages=min(16, page_table.shape[1]))
    y_sample = _post_call(xs_rows, yssd_s[None], yatt_s.reshape(1, nbs, D_ATT), gates_s, ms[2], ms[3], ms[4], ms[5],
                          gmlp, gfin, wps, wpa, wo, wup, wdn, tm=nbs)

    hshape = (N_ATT_HEADS, ATT_HEAD_DIM)
    return (y_prompt,
            y_sample.reshape(nbs, 1, D_MODEL),
            k_p.reshape(1, nbp, L, *hshape),
            v_p.reshape(1, nbp, L, *hshape),
            lf_p[None, :, :, :N_ATT_HEADS],
            conv_p[None],
            ssm_p[None],
            k_s.reshape(1, nbs, 1, *hshape),
            v_s.reshape(1, nbs, 1, *hshape),
            logf_s.reshape(1, nbs, 1, N_ATT_HEADS),
            jnp.transpose(conv_s[0], (1, 0, 2))[None],
            ssm_s[None])
```
